```python
import math
import jax, jax.numpy as jnp
from jax import lax
import numpy as np

D_MODEL = 1024
BATCH = 8
SEQ = 4096
DEPTH = 2

CHUNK = 64
MEM_LEN = 256
EPS = 1e-6
A_WIDTH = 512
A_GROUPS = 4
A_GROUP_DIM = A_WIDTH // A_GROUPS
GMLP_BLOCK = 128
B_WIDTH = 512
CONV_WIDTH = 31
MIX_WIDTH = A_WIDTH + B_WIDTH
IN_WIDTH = 2 * A_WIDTH + 2 * B_WIDTH
C_WIDTH = 512
C_GROUP_CH = 16
C_GROUPS = C_WIDTH // C_GROUP_CH
C_STATE = 64
DT_MIN = 1e-3
DT_MAX = 1e-1
CA_HEADS = 4
CA_HEAD_DIM = D_MODEL // CA_HEADS
FFN_HIDDEN = -(-8 * D_MODEL // (3 * 256)) * 256
N_EVEN = (DEPTH + 1) // 2
N_ODD = DEPTH // 2

kernel_name = "chunk_causal_hybrid_gmlp_conformer_s5_trunk"


def rmsnorm(x, g):
    xf = x.astype(jnp.float32)
    y = xf * lax.rsqrt(jnp.mean(xf * xf, axis=-1, keepdims=True) + EPS)
    return (y * g.astype(jnp.float32)).astype(x.dtype)


def layernorm(x, g=None, b=None):
    xf = x.astype(jnp.float32)
    mu = jnp.mean(xf, axis=-1, keepdims=True)
    xc = xf - mu
    y = xc * lax.rsqrt(jnp.mean(xc * xc, axis=-1, keepdims=True) + EPS)
    if g is not None:
        y = y * g.astype(jnp.float32) + b.astype(jnp.float32)
    return y.astype(x.dtype)


def gmlp_spatial_gate(u, v, w_s, b_s):
    bn, s, _ = v.shape
    v = layernorm(v)
    v = v.reshape(bn, s // GMLP_BLOCK, GMLP_BLOCK, A_GROUPS, A_GROUP_DIM)
    chunk_id = jnp.arange(GMLP_BLOCK) // CHUNK
    mask = chunk_id[None, :] <= chunk_id[:, None]
    w = jnp.where(mask[None], w_s, jnp.zeros_like(w_s))
    sg = jnp.einsum('gij,bnjgc->bnigc', w, v) + b_s.T[None, None, :, :, None]
    return u * sg.reshape(bn, s, A_WIDTH)


def conformer_conv(a, g, conv_w, conv_b, ln_g, ln_b):
    h = a * jax.nn.sigmoid(g)
    h = lax.conv_general_dilated(
        h, conv_w[:, None, :].astype(h.dtype), window_strides=(1,),
        padding=[(CONV_WIDTH - 1, 0)], dimension_numbers=('NWC', 'WIO', 'NWC'),
        feature_group_count=B_WIDTH) + conv_b
    h = layernorm(h, ln_g, ln_b)
    return jax.nn.silu(h)


def _complex_affine_combine(e1, e2):
    a1r, a1i, b1r, b1i = e1
    a2r, a2i, b2r, b2i = e2
    ar = a1r * a2r - a1i * a2i
    ai = a1r * a2i + a1i * a2r
    br = a2r * b1r - a2i * b1i + b2r
    bi = a2r * b1i + a2i * b1r + b2i
    return (ar, ai, br, bi)


def s5_layer(u, lam_re, lam_im, log_dt, b_re, b_im, c_re, c_im, d_skip):
    bn, s, _ = u.shape
    f32 = jnp.float32
    uf = u.astype(f32)
    dt = jnp.exp(log_dt.astype(f32))[:, None]
    lr = lam_re.astype(f32)
    li = lam_im.astype(f32)
    mag = jnp.exp(lr * dt)
    ar = mag * jnp.cos(li * dt)
    ai = mag * jnp.sin(li * dt)
    den = lr * lr + li * li
    qr = ((ar - 1.0) * lr + ai * li) / den
    qi = (ai * lr - (ar - 1.0) * li) / den
    br_ = b_re.astype(f32)
    bi_ = b_im.astype(f32)
    bbr = qr[..., None] * br_ - qi[..., None] * bi_
    bbi = qr[..., None] * bi_ + qi[..., None] * br_
    ug = uf.reshape(bn, s, C_GROUPS, C_GROUP_CH).transpose(1, 0, 2, 3)
    bu_r = jnp.einsum('sbgc,gpc->sbgp', ug, bbr)
    bu_i = jnp.einsum('sbgc,gpc->sbgp', ug, bbi)
    a_r = jnp.broadcast_to(ar[None, None], (s, 1, C_GROUPS, C_STATE))
    a_i = jnp.broadcast_to(ai[None, None], (s, 1, C_GROUPS, C_STATE))
    _, _, xr, xi = lax.associative_scan(_complex_affine_combine, (a_r, a_i, bu_r, bu_i), axis=0)
    y = (jnp.einsum('sbgp,gcp->sbgc', xr, c_re.astype(f32))
         - jnp.einsum('sbgp,gcp->sbgc', xi, c_im.astype(f32)))
    y = y.transpose(1, 0, 2, 3).reshape(bn, s, C_WIDTH) + d_skip.astype(f32) * uf
    return y.astype(u.dtype)


def cross_attention(xn, memn, wq, wk, wv, wo):
    bn, s, _ = xn.shape
    m = memn.shape[1]
    q = (xn @ wq).reshape(bn, s, CA_HEADS, CA_HEAD_DIM)
    k = (memn @ wk).reshape(bn, m, CA_HEADS, CA_HEAD_DIM)
    v = (memn @ wv).reshape(bn, m, CA_HEADS, CA_HEAD_DIM)
    sc = jnp.einsum('bshd,bmhd->bhsm', q, k).astype(jnp.float32) * (CA_HEAD_DIM ** -0.5)
    p = jax.nn.softmax(sc, axis=-1).astype(v.dtype)
    o = jnp.einsum('bhsm,bmhd->bshd', p, v).reshape(bn, s, D_MODEL)
    return o @ wo


def swiglu(xn, wg, wu, wd):
    return (jax.nn.silu(xn @ wg) * (xn @ wu)) @ wd


def setup_inputs(seed: int = 0) -> dict:
    key = jax.random.key(seed)
    ks = iter(jax.random.split(key, 48))

    def nrm(shape, scale):
        return jax.random.normal(next(ks), shape, jnp.float32) * scale

    def gain(shape):
        return 1.0 + nrm(shape, 0.02)

    d, h = D_MODEL, FFN_HIDDEN
    inp = {}
    inp['x'] = nrm((BATCH, SEQ, d), 1.0)
    inp['mem'] = nrm((BATCH, MEM_LEN, d), 1.0)
    inp['e_norm'] = gain((N_EVEN, d))
    inp['e_w_in'] = nrm((N_EVEN, d, IN_WIDTH), d ** -0.5)
    inp['e_gmlp_w'] = nrm((N_EVEN, A_GROUPS, GMLP_BLOCK, GMLP_BLOCK), 0.5 * GMLP_BLOCK ** -0.5)
    inp['e_gmlp_b'] = gain((N_EVEN, A_GROUPS, GMLP_BLOCK))
    inp['e_conv_w'] = nrm((N_EVEN, CONV_WIDTH, B_WIDTH), CONV_WIDTH ** -0.5)
    inp['e_conv_b'] = nrm((N_EVEN, B_WIDTH), 0.02)
    inp['e_conv_ln_g'] = gain((N_EVEN, B_WIDTH))
    inp['e_conv_ln_b'] = nrm((N_EVEN, B_WIDTH), 0.02)
    inp['e_w_out'] = nrm((N_EVEN, MIX_WIDTH, d), MIX_WIDTH ** -0.5)
    inp['o_norm'] = gain((N_ODD, d))
    inp['o_w_in'] = nrm((N_ODD, d, C_WIDTH), d ** -0.5)
    inp['o_lam_re'] = -0.5 + nrm((N_ODD, C_GROUPS, C_STATE), 0.01)
    inp['o_lam_im'] = (math.pi * jnp.arange(C_STATE, dtype=jnp.float32))[None, None, :] + nrm((N_ODD, C_GROUPS, C_STATE), 0.01)
    inp['o_log_dt'] = jax.random.uniform(next(ks), (N_ODD, C_GROUPS), jnp.float32, math.log(DT_MIN), math.log(DT_MAX))
    inp['o_b_re'] = nrm((N_ODD, C_GROUPS, C_STATE, C_GROUP_CH), (2 * C_GROUP_CH) ** -0.5)
    inp['o_b_im'] = nrm((N_ODD, C_GROUPS, C_STATE, C_GROUP_CH), (2 * C_GROUP_CH) ** -0.5)
    inp['o_c_re'] = nrm((N_ODD, C_GROUPS, C_GROUP_CH, C_STATE), (2 * C_STATE) ** -0.5)
    inp['o_c_im'] = nrm((N_ODD, C_GROUPS, C_GROUP_CH, C_STATE), (2 * C_STATE) ** -0.5)
    inp['o_d'] = gain((N_ODD, C_WIDTH))
    inp['o_w_out'] = nrm((N_ODD, C_WIDTH, 2 * d), C_WIDTH ** -0.5)
    inp['ca_norm'] = gain((DEPTH, d))
    inp['ca_mem_norm'] = gain((DEPTH, d))
    inp['ca_wq'] = nrm((DEPTH, d, d), d ** -0.5)
    inp['ca_wk'] = nrm((DEPTH, d, d), d ** -0.5)
    inp['ca_wv'] = nrm((DEPTH, d, d), d ** -0.5)
    inp['ca_wo'] = nrm((DEPTH, d, d), d ** -0.5)
    inp['ffn_norm'] = gain((DEPTH, d))
    inp['ffn_w_gate'] = nrm((DEPTH, d, h), d ** -0.5)
    inp['ffn_w_up'] = nrm((DEPTH, d, h), d ** -0.5)
    inp['ffn_w_down'] = nrm((DEPTH, h, d), h ** -0.5)
    inp['final_norm'] = gain((d,))
    return inp


def reference(x, mem, e_norm, e_w_in, e_gmlp_w, e_gmlp_b, e_conv_w, e_conv_b, e_conv_ln_g,
              e_conv_ln_b, e_w_out, o_norm, o_w_in, o_lam_re, o_lam_im, o_log_dt, o_b_re,
              o_b_im, o_c_re, o_c_im, o_d, o_w_out, ca_norm, ca_mem_norm, ca_wq, ca_wk, ca_wv,
              ca_wo, ffn_norm, ffn_w_gate, ffn_w_up, ffn_w_down, final_norm):
    for i in range(DEPTH):
        j = i // 2
        if i % 2 == 0:
            hn = rmsnorm(x, e_norm[j])
            proj = hn @ e_w_in[j]
            a_u, a_v, b_a, b_g = jnp.split(proj, [A_WIDTH, 2 * A_WIDTH, 2 * A_WIDTH + B_WIDTH], axis=-1)
            out_a = gmlp_spatial_gate(jax.nn.gelu(a_u), jax.nn.gelu(a_v), e_gmlp_w[j], e_gmlp_b[j])
            out_b = conformer_conv(b_a, b_g, e_conv_w[j], e_conv_b[j], e_conv_ln_g[j], e_conv_ln_b[j])
            mix = jnp.concatenate([out_a, out_b], axis=-1) @ e_w_out[j]
        else:
            hn = rmsnorm(x, o_norm[j])
            u = hn @ o_w_in[j]
            y = s5_layer(u, o_lam_re[j], o_lam_im[j], o_log_dt[j], o_b_re[j], o_b_im[j],
                         o_c_re[j], o_c_im[j], o_d[j])
            o = jax.nn.gelu(y) @ o_w_out[j]
            mix = o[..., :D_MODEL] * jax.nn.sigmoid(o[..., D_MODEL:])
        x = x + mix
        x = x + cross_attention(rmsnorm(x, ca_norm[i]), rmsnorm(mem, ca_mem_norm[i]),
                                ca_wq[i], ca_wk[i], ca_wv[i], ca_wo[i])
        x = x + swiglu(rmsnorm(x, ffn_norm[i]), ffn_w_gate[i], ffn_w_up[i], ffn_w_down[i])
    return rmsnorm(x, final_norm)
```

```python
import functools
import math

import jax
import jax.numpy as jnp
from jax import lax
from jax.experimental import pallas as pl
from jax.experimental.pallas import tpu as pltpu

F32 = jnp.float32
BF16 = jnp.bfloat16

D_MODEL = 1024
BATCH = 8
SEQ = 4096
DEPTH = 2
CHUNK = 64
MEM_LEN = 256
EPS = 1e-6
A_WIDTH = 512
A_GROUPS = 4
A_GROUP_DIM = A_WIDTH // A_GROUPS
GMLP_BLOCK = 128
B_WIDTH = 512
CONV_WIDTH = 31
MIX_WIDTH = A_WIDTH + B_WIDTH
IN_WIDTH = 2 * A_WIDTH + 2 * B_WIDTH
C_WIDTH = 512
C_GROUP_CH = 16
C_GROUPS = C_WIDTH // C_GROUP_CH
C_STATE = 64
S_LANES = C_GROUPS * C_STATE
CA_HEADS = 4
CA_HEAD_DIM = D_MODEL // CA_HEADS
FFN_HIDDEN = -(-8 * D_MODEL // (3 * 256)) * 256

VMEM_LIMIT_BYTES = 56 * 1024 * 1024
SUBLANES = 8
LANES = 128

TOKEN_TILE = 512
CONV_HALO = 32
CONV_ROWS = 64
SCAN_STEPS = 64
SCAN_LANES = 512
FFN_CHUNKS = 2


def _params(n_axes):
    return pltpu.CompilerParams(
        dimension_semantics=("arbitrary",) * n_axes,
        vmem_limit_bytes=VMEM_LIMIT_BYTES)


def _resident(shape):
    return pl.BlockSpec(shape, lambda *_: (0,) * len(shape),
                        pipeline_mode=pl.Buffered(1))


def _tok_spec(width):
    return pl.BlockSpec((None, TOKEN_TILE, width), lambda b, s: (b, s, 0))


def _rms(x, g):
    return x * lax.rsqrt(jnp.mean(x * x, axis=-1, keepdims=True) + EPS) * g


def _ln(x):
    mu = jnp.mean(x, axis=-1, keepdims=True)
    xc = x - mu
    return xc * lax.rsqrt(jnp.mean(xc * xc, axis=-1, keepdims=True) + EPS)


def _mm(a, b):
    return jnp.dot(a, b, preferred_element_type=F32)


def _even_kernel(x_ref, g_ref, win_ref, gw_ref, gb_ref, cw_ref, cb_ref, lng_ref,
                 lnb_ref, wout_ref, o_ref, hext_ref, conv_ref):
    tm = TOKEN_TILE
    nblk = tm // GMLP_BLOCK
    x = x_ref[...]
    hn = _rms(x, g_ref[...]).astype(BF16)
    proj = _mm(hn, win_ref[...])

    u = jax.nn.gelu(proj[:, :A_WIDTH])
    v = _ln(jax.nn.gelu(proj[:, A_WIDTH:2 * A_WIDTH])).astype(BF16)
    ri = lax.broadcasted_iota(jnp.int32, (GMLP_BLOCK, GMLP_BLOCK), 0) >> 6
    ci = lax.broadcasted_iota(jnp.int32, (GMLP_BLOCK, GMLP_BLOCK), 1) >> 6
    keep = ci <= ri
    per_group = []
    for g in range(A_GROUPS):
        gs = slice(g * A_GROUP_DIM, (g + 1) * A_GROUP_DIM)
        rhs = jnp.concatenate(
            [v[nb * GMLP_BLOCK:(nb + 1) * GMLP_BLOCK, gs] for nb in range(nblk)], axis=1)
        wm = jnp.where(keep, gw_ref[g], 0.0).astype(BF16)
        per_group.append(_mm(wm, rhs))
    sg = jnp.concatenate(
        [jnp.concatenate([per_group[g][:, nb * A_GROUP_DIM:(nb + 1) * A_GROUP_DIM]
                          for g in range(A_GROUPS)], axis=1) for nb in range(nblk)], axis=0)
    bias = jnp.concatenate([gb_ref[...]] * nblk, axis=0)
    out_a = u * (sg + bias)

    h = proj[:, 2 * A_WIDTH:2 * A_WIDTH + B_WIDTH] * jax.nn.sigmoid(proj[:, 2 * A_WIDTH + B_WIDTH:])

    @pl.when(pl.program_id(1) == 0)
    def _():
        hext_ref[0:CONV_HALO, :] = jnp.zeros((CONV_HALO, B_WIDTH), F32)

    hext_ref[CONV_HALO:CONV_HALO + tm, :] = h
    for cb in range(B_WIDTH // LANES):
        cs = slice(cb * LANES, (cb + 1) * LANES)
        for rc in range(tm // CONV_ROWS):
            base = CONV_HALO + rc * CONV_ROWS
            acc = jnp.zeros((CONV_ROWS, LANES), F32)
            for lag in range(CONV_WIDTH):
                k = CONV_WIDTH - 1 - lag
                acc = acc + cw_ref[k:k + 1, cs] * hext_ref[base - lag:base - lag + CONV_ROWS, cs]
            conv_ref[rc * CONV_ROWS:(rc + 1) * CONV_ROWS, cs] = acc
    hext_ref[0:CONV_HALO, :] = hext_ref[tm:tm + CONV_HALO, :]
    c = _ln(conv_ref[...] + cb_ref[...]) * lng_ref[...] + lnb_ref[...]
    out_b = jax.nn.silu(c)

    mix = (_mm(out_a.astype(BF16), wout_ref[0:A_WIDTH, :])
           + _mm(out_b.astype(BF16), wout_ref[A_WIDTH:MIX_WIDTH, :]))
    o_ref[...] = x + mix


def _even_layer(x, norm, w_in, gmlp_w, gmlp_b, conv_w, conv_b, ln_g, ln_b, w_out):
    gb_full = jnp.repeat(gmlp_b.T, A_GROUP_DIM, axis=1)
    row = lambda a: a.reshape(1, -1)
    return pl.pallas_call(
        _even_kernel,
        grid=(BATCH, SEQ // TOKEN_TILE),
        in_specs=[
            _tok_spec(D_MODEL),
            _resident((1, D_MODEL)),
            _resident((D_MODEL, IN_WIDTH)),
            _resident((A_GROUPS, GMLP_BLOCK, GMLP_BLOCK)),
            _resident((GMLP_BLOCK, A_WIDTH)),
            _resident((CONV_WIDTH, B_WIDTH)),
            _resident((1, B_WIDTH)),
            _resident((1, B_WIDTH)),
            _resident((1, B_WIDTH)),
            _resident((MIX_WIDTH, D_MODEL)),
        ],
        out_specs=_tok_spec(D_MODEL),
        out_shape=jax.ShapeDtypeStruct((BATCH, SEQ, D_MODEL), F32),
        scratch_shapes=[pltpu.VMEM((CONV_HALO + TOKEN_TILE, B_WIDTH), F32),
                        pltpu.VMEM((TOKEN_TILE, B_WIDTH), F32)],
        compiler_params=_params(2),
        name="even_mixer",
    )(x, row(norm), w_in.astype(BF16), gmlp_w, gb_full, conv_w, row(conv_b), row(ln_g),
      row(ln_b), w_out.astype(BF16))


def _odd_kernel(x_ref, g_ref, win_ref, bblk_ref, ar_ref, ai_ref, cblk_ref, d_ref, wout_ref,
                o_ref, xs_ref, st_ref):
    steps = SCAN_STEPS
    rows = steps * BATCH

    @pl.when(pl.program_id(0) == 0)
    def _():
        st_ref[...] = jnp.zeros(st_ref.shape, F32)

    x = x_ref[...].reshape(rows, D_MODEL)
    u_bt = _mm(_rms(x, g_ref[...]).astype(BF16), win_ref[...])
    u = pltpu.einshape("btc->tbc", u_bt.reshape(BATCH, steps, C_WIDTH)).reshape(rows, C_WIDTH)
    xs_ref[...] = _mm(u.astype(BF16), bblk_ref[...])

    for lb in range(S_LANES // SCAN_LANES):
        re = slice(lb * SCAN_LANES, (lb + 1) * SCAN_LANES)
        im = slice(S_LANES + lb * SCAN_LANES, S_LANES + (lb + 1) * SCAN_LANES)
        ar = jnp.broadcast_to(ar_ref[:, re], (BATCH, SCAN_LANES))
        ai = jnp.broadcast_to(ai_ref[:, re], (BATCH, SCAN_LANES))

        def step(t, carry):
            sr, si = carry
            r = pl.ds(pl.multiple_of(t * BATCH, BATCH), BATCH)
            nr = ar * sr - ai * si + xs_ref[r, re]
            ni = ar * si + ai * sr + xs_ref[r, im]
            xs_ref[r, re] = nr
            xs_ref[r, im] = ni
            return nr, ni

        sr, si = lax.fori_loop(0, steps, step, (st_ref[:, re], st_ref[:, im]), unroll=4)
        st_ref[:, re] = sr
        st_ref[:, im] = si

    y = _mm(xs_ref[...].astype(BF16), cblk_ref[...]) + d_ref[...] * u
    y_bt = pltpu.einshape("tbc->btc", y.reshape(steps, BATCH, C_WIDTH)).reshape(rows, C_WIDTH)
    o = _mm(jax.nn.gelu(y_bt).astype(BF16), wout_ref[...])
    out = x + o[:, :D_MODEL] * jax.nn.sigmoid(o[:, D_MODEL:])
    o_ref[...] = out.reshape(BATCH, steps, D_MODEL)


def _s5_discretize(lam_re, lam_im, log_dt, b_re, b_im, c_re, c_im):
    dt = jnp.exp(log_dt)[:, None]
    mag = jnp.exp(lam_re * dt)
    ar = mag * jnp.cos(lam_im * dt)
    ai = mag * jnp.sin(lam_im * dt)
    den = lam_re * lam_re + lam_im * lam_im
    qr = ((ar - 1.0) * lam_re + ai * lam_im) / den
    qi = (ai * lam_re - (ar - 1.0) * lam_im) / den
    bbr = qr[..., None] * b_re - qi[..., None] * b_im
    bbi = qr[..., None] * b_im + qi[..., None] * b_re
    eye = jnp.eye(C_GROUPS, dtype=F32)
    blk_b = lambda m: jnp.einsum('gpc,gh->gchp', m, eye).reshape(C_WIDTH, S_LANES)
    blk_c = lambda m: jnp.einsum('gcp,gh->gphc', m, eye).reshape(S_LANES, C_WIDTH)
    bblk = jnp.concatenate([blk_b(bbr), blk_b(bbi)], axis=1)
    cblk = jnp.concatenate([blk_c(c_re), -blk_c(c_im)], axis=0)
    return (bblk.astype(BF16), ar.reshape(1, S_LANES), ai.reshape(1, S_LANES), cblk.astype(BF16))


def _odd_layer(x, norm, w_in, ssm, d_skip, w_out):
    bblk, ar, ai, cblk = ssm
    spec = pl.BlockSpec((BATCH, SCAN_STEPS, D_MODEL), lambda i: (0, i, 0))
    return pl.pallas_call(
        _odd_kernel,
        grid=(SEQ // SCAN_STEPS,),
        in_specs=[
            spec,
            _resident((1, D_MODEL)),
            _resident((D_MODEL, C_WIDTH)),
            _resident((C_WIDTH, 2 * S_LANES)),
            _resident((1, S_LANES)),
            _resident((1, S_LANES)),
            _resident((2 * S_LANES, C_WIDTH)),
            _resident((1, C_WIDTH)),
            _resident((C_WIDTH, 2 * D_MODEL)),
        ],
        out_specs=spec,
        out_shape=jax.ShapeDtypeStruct((BATCH, SEQ, D_MODEL), F32),
        scratch_shapes=[pltpu.VMEM((SCAN_STEPS * BATCH, 2 * S_LANES), F32),
                        pltpu.VMEM((BATCH, 2 * S_LANES), F32)],
        compiler_params=_params(1),
        name="odd_mixer",
    )(x, norm.reshape(1, -1), w_in.astype(BF16), bblk, ar, ai, cblk, d_skip.reshape(1, -1),
      w_out.astype(BF16))


def _kv_kernel(m_ref, g_ref, wk_ref, wv_ref, k_ref, v_ref):
    mn = _rms(m_ref[...], g_ref[...]).astype(BF16)
    k_ref[...] = _mm(mn, wk_ref[...]).astype(BF16)
    v_ref[...] = _mm(mn, wv_ref[...]).astype(BF16)


def _kv_proj(mem, norm, wk, wv):
    spec = pl.BlockSpec((None, MEM_LEN, D_MODEL), lambda b: (b, 0, 0))
    shape = jax.ShapeDtypeStruct((BATCH, MEM_LEN, D_MODEL), BF16)
    return pl.pallas_call(
        _kv_kernel,
        grid=(BATCH,),
        in_specs=[spec, _resident((1, D_MODEL)), _resident((D_MODEL, D_MODEL)),
                  _resident((D_MODEL, D_MODEL))],
        out_specs=[spec, spec],
        out_shape=[shape, shape],
        compiler_params=_params(1),
        name="ca_kv_proj",
    )(mem, norm.reshape(1, -1), wk.astype(BF16), wv.astype(BF16))


def _ca_kernel(x_ref, g_ref, wq_ref, k_ref, v_ref, wo_ref, o_ref):
    x = x_ref[...]
    q = _mm(_rms(x, g_ref[...]).astype(BF16), wq_ref[...]).astype(BF16)
    heads = []
    for h in range(CA_HEADS):
        hs = slice(h * CA_HEAD_DIM, (h + 1) * CA_HEAD_DIM)
        sc = lax.dot_general(q[:, hs], k_ref[:, hs], (((1,), (1,)), ((), ())),
                             preferred_element_type=F32) * (CA_HEAD_DIM ** -0.5)
        e = jnp.exp(sc - jnp.max(sc, axis=-1, keepdims=True))
        p = e / jnp.sum(e, axis=-1, keepdims=True)
        heads.append(_mm(p.astype(BF16), v_ref[:, hs]))
    o = jnp.concatenate(heads, axis=1).astype(BF16)
    o_ref[...] = x + _mm(o, wo_ref[...])


def _cross_attention(x, k, v, norm, wq, wo):
    kv_spec = pl.BlockSpec((None, MEM_LEN, D_MODEL), lambda b, s: (b, 0, 0))
    return pl.pallas_call(
        _ca_kernel,
        grid=(BATCH, SEQ // TOKEN_TILE),
        in_specs=[_tok_spec(D_MODEL), _resident((1, D_MODEL)), _resident((D_MODEL, D_MODEL)),
                  kv_spec, kv_spec, _resident((D_MODEL, D_MODEL))],
        out_specs=_tok_spec(D_MODEL),
        out_shape=jax.ShapeDtypeStruct((BATCH, SEQ, D_MODEL), F32),
        compiler_params=_params(2),
        name="cross_attention",
    )(x, norm.reshape(1, -1), wq.astype(BF16), k, v, wo.astype(BF16))


def _ffn_kernel(x_ref, g_ref, wg_ref, wu_ref, wd_ref, fg_ref, o_ref, *, final):
    x = x_ref[...]
    xn = _rms(x, g_ref[...]).astype(BF16)
    hc = FFN_HIDDEN // FFN_CHUNKS
    y = x
    for c in range(FFN_CHUNKS):
        cs = slice(c * hc, (c + 1) * hc)
        h = jax.nn.silu(_mm(xn, wg_ref[:, cs])) * _mm(xn, wu_ref[:, cs])
        y = y + _mm(h.astype(BF16), wd_ref[cs, :])
    if final:
        y = _rms(y, fg_ref[...])
    o_ref[...] = y


def _ffn(x, norm, wg, wu, wd, final_norm, final):
    return pl.pallas_call(
        functools.partial(_ffn_kernel, final=final),
        grid=(BATCH, SEQ // TOKEN_TILE),
        in_specs=[_tok_spec(D_MODEL), _resident((1, D_MODEL)),
                  _resident((D_MODEL, FFN_HIDDEN)), _resident((D_MODEL, FFN_HIDDEN)),
                  _resident((FFN_HIDDEN, D_MODEL)), _resident((1, D_MODEL))],
        out_specs=_tok_spec(D_MODEL),
        out_shape=jax.ShapeDtypeStruct((BATCH, SEQ, D_MODEL), F32),
        compiler_params=_params(2),
        name="ffn_final" if final else "ffn",
    )(x, norm.reshape(1, -1), wg.astype(BF16), wu.astype(BF16), wd.astype(BF16),
      final_norm.reshape(1, -1))


def kernel(x, mem, e_norm, e_w_in, e_gmlp_w, e_gmlp_b, e_conv_w, e_conv_b, e_conv_ln_g, e_conv_ln_b, e_w_out, o_norm, o_w_in, o_lam_re, o_lam_im, o_log_dt, o_b_re, o_b_im, o_c_re, o_c_im, o_d, o_w_out, ca_norm, ca_mem_norm, ca_wq, ca_wk, ca_wv, ca_wo, ffn_norm, ffn_w_gate, ffn_w_up, ffn_w_down, final_norm):
    for i in range(DEPTH):
        j = i // 2
        if i % 2 == 0:
            x = _even_layer(x, e_norm[j], e_w_in[j], e_gmlp_w[j], e_gmlp_b[j], e_conv_w[j],
                            e_conv_b[j], e_conv_ln_g[j], e_conv_ln_b[j], e_w_out[j])
        else:
            ssm = _s5_discretize(o_lam_re[j], o_lam_im[j], o_log_dt[j], o_b_re[j], o_b_im[j],
                                 o_c_re[j], o_c_im[j])
            x = _odd_layer(x, o_norm[j], o_w_in[j], ssm, o_d[j], o_w_out[j])
        k, v = _kv_proj(mem, ca_mem_norm[i], ca_wk[i], ca_wv[i])
        x = _cross_attention(x, k, v, ca_norm[i], ca_wq[i], ca_wo[i])
        x = _ffn(x, ffn_norm[i], ffn_w_gate[i], ffn_w_up[i], ffn_w_down[i], final_norm,
                 final=(i == DEPTH - 1))
    return x
```

```python
import functools
import math

import jax
import jax.numpy as jnp
from jax import lax
from jax.experimental import pallas as pl
from jax.experimental.pallas import tpu as pltpu

F32 = jnp.float32
BF16 = jnp.bfloat16

D_MODEL = 1024
BATCH = 8
SEQ = 4096
DEPTH = 2
CHUNK = 64
MEM_LEN = 256
EPS = 1e-6
A_WIDTH = 512
A_GROUPS = 4
A_GROUP_DIM = A_WIDTH // A_GROUPS
GMLP_BLOCK = 128
B_WIDTH = 512
CONV_WIDTH = 31
MIX_WIDTH = A_WIDTH + B_WIDTH
IN_WIDTH = 2 * A_WIDTH + 2 * B_WIDTH
C_WIDTH = 512
C_GROUP_CH = 16
C_GROUPS = C_WIDTH // C_GROUP_CH
C_STATE = 64
S_LANES = C_GROUPS * C_STATE
S5_SPLIT = 2
S5_CH = C_WIDTH // S5_SPLIT
S5_ST = S_LANES // S5_SPLIT
CA_HEADS = 4
CA_HEAD_DIM = D_MODEL // CA_HEADS
FFN_HIDDEN = -(-8 * D_MODEL // (3 * 256)) * 256

VMEM_LIMIT_BYTES = 56 * 1024 * 1024
SUBLANES = 8
LANES = 128

TOKEN_TILE = 512
CONV_HALO = 32
CONV_ROWS = 128
SCAN_STEPS = 64
SCAN_LANES = 512
FFN_CHUNKS = 2


def _params(n_axes):
    return pltpu.CompilerParams(
        dimension_semantics=("arbitrary",) * n_axes,
        vmem_limit_bytes=VMEM_LIMIT_BYTES)


def _resident(shape):
    return pl.BlockSpec(shape, lambda *_: (0,) * len(shape),
                        pipeline_mode=pl.Buffered(1))


def _tok_spec(width):
    return pl.BlockSpec((None, TOKEN_TILE, width), lambda b, s: (b, s, 0))


def _rms(x, g):
    return x * lax.rsqrt(jnp.mean(x * x, axis=-1, keepdims=True) + EPS) * g


def _ln(x):
    mu = jnp.mean(x, axis=-1, keepdims=True)
    xc = x - mu
    return xc * lax.rsqrt(jnp.mean(xc * xc, axis=-1, keepdims=True) + EPS)


def _mm(a, b):
    return jnp.dot(a, b, preferred_element_type=F32)


def _even_kernel(x_ref, g_ref, win_ref, gw_ref, gb_ref, cw_ref, cb_ref, lng_ref,
                 lnb_ref, wout_ref, o_ref, hext_ref, conv_ref):
    tm = TOKEN_TILE
    nblk = tm // GMLP_BLOCK
    x = x_ref[...]
    hn = _rms(x, g_ref[...]).astype(BF16)
    proj = _mm(hn, win_ref[...])

    u = jax.nn.gelu(proj[:, :A_WIDTH])
    v = _ln(jax.nn.gelu(proj[:, A_WIDTH:2 * A_WIDTH])).astype(BF16)
    ri = lax.broadcasted_iota(jnp.int32, (GMLP_BLOCK, GMLP_BLOCK), 0) >> 6
    ci = lax.broadcasted_iota(jnp.int32, (GMLP_BLOCK, GMLP_BLOCK), 1) >> 6
    keep = ci <= ri
    per_group = []
    for g in range(A_GROUPS):
        gs = slice(g * A_GROUP_DIM, (g + 1) * A_GROUP_DIM)
        rhs = jnp.concatenate(
            [v[nb * GMLP_BLOCK:(nb + 1) * GMLP_BLOCK, gs] for nb in range(nblk)], axis=1)
        wm = jnp.where(keep, gw_ref[g], 0.0).astype(BF16)
        per_group.append(_mm(wm, rhs))
    sg = jnp.concatenate(
        [jnp.concatenate([per_group[g][:, nb * A_GROUP_DIM:(nb + 1) * A_GROUP_DIM]
                          for g in range(A_GROUPS)], axis=1) for nb in range(nblk)], axis=0)
    bias = jnp.concatenate([gb_ref[...]] * nblk, axis=0)
    out_a = u * (sg + bias)

    h = proj[:, 2 * A_WIDTH:2 * A_WIDTH + B_WIDTH] * jax.nn.sigmoid(proj[:, 2 * A_WIDTH + B_WIDTH:])

    @pl.when(pl.program_id(1) == 0)
    def _():
        hext_ref[0:CONV_HALO, :] = jnp.zeros((CONV_HALO, B_WIDTH), F32)

    hext_ref[CONV_HALO:CONV_HALO + tm, :] = h
    for cb in range(B_WIDTH // LANES):
        cs = slice(cb * LANES, (cb + 1) * LANES)
        for rc in range(tm // CONV_ROWS):
            r0 = rc * CONV_ROWS
            win = hext_ref[r0:r0 + CONV_ROWS + CONV_HALO, cs]
            acc = jnp.zeros((CONV_ROWS, LANES), F32)
            for r in range(SUBLANES):
                shifted = win if r == 0 else pltpu.roll(win, r, axis=0)
                for q in range(CONV_HALO // SUBLANES):
                    lag = SUBLANES * q + r
                    if lag < CONV_WIDTH:
                        k = CONV_WIDTH - 1 - lag
                        lo = CONV_HALO - SUBLANES * q
                        acc = acc + cw_ref[k:k + 1, cs] * shifted[lo:lo + CONV_ROWS, :]
            conv_ref[r0:r0 + CONV_ROWS, cs] = acc
    hext_ref[0:CONV_HALO, :] = hext_ref[tm:tm + CONV_HALO, :]
    c = _ln(conv_ref[...] + cb_ref[...]) * lng_ref[...] + lnb_ref[...]
    out_b = jax.nn.silu(c)

    mix = (_mm(out_a.astype(BF16), wout_ref[0:A_WIDTH, :])
           + _mm(out_b.astype(BF16), wout_ref[A_WIDTH:MIX_WIDTH, :]))
    o_ref[...] = x + mix


def _even_layer(x, norm, w_in, gmlp_w, gmlp_b, conv_w, conv_b, ln_g, ln_b, w_out):
    gb_full = jnp.repeat(gmlp_b.T, A_GROUP_DIM, axis=1)
    row = lambda a: a.reshape(1, -1)
    return pl.pallas_call(
        _even_kernel,
        grid=(BATCH, SEQ // TOKEN_TILE),
        in_specs=[
            _tok_spec(D_MODEL),
            _resident((1, D_MODEL)),
            _resident((D_MODEL, IN_WIDTH)),
            _resident((A_GROUPS, GMLP_BLOCK, GMLP_BLOCK)),
            _resident((GMLP_BLOCK, A_WIDTH)),
            _resident((CONV_WIDTH, B_WIDTH)),
            _resident((1, B_WIDTH)),
            _resident((1, B_WIDTH)),
            _resident((1, B_WIDTH)),
            _resident((MIX_WIDTH, D_MODEL)),
        ],
        out_specs=_tok_spec(D_MODEL),
        out_shape=jax.ShapeDtypeStruct((BATCH, SEQ, D_MODEL), F32),
        scratch_shapes=[pltpu.VMEM((CONV_HALO + TOKEN_TILE, B_WIDTH), F32),
                        pltpu.VMEM((TOKEN_TILE, B_WIDTH), F32)],
        compiler_params=_params(2),
        name="even_mixer",
    )(x, row(norm), w_in.astype(BF16), gmlp_w, gb_full, conv_w, row(conv_b), row(ln_g),
      row(ln_b), w_out.astype(BF16))


def _odd_kernel(x_ref, g_ref, win_ref, bblk_ref, ar_ref, ai_ref, cblk_ref, d_ref, wout_ref,
                o_ref, xs_ref, st_ref):
    steps = SCAN_STEPS
    rows = steps * BATCH

    @pl.when(pl.program_id(0) == 0)
    def _():
        st_ref[...] = jnp.zeros(st_ref.shape, F32)

    x = x_ref[...].reshape(rows, D_MODEL)
    u_bt = _mm(_rms(x, g_ref[...]).astype(BF16), win_ref[...])
    u = jnp.swapaxes(u_bt.reshape(BATCH, steps, C_WIDTH), 0, 1).reshape(rows, C_WIDTH)
    ub = u.astype(BF16)
    for m in range(S5_SPLIT):
        bu = _mm(ub[:, m * S5_CH:(m + 1) * S5_CH], bblk_ref[m])
        xs_ref[:, m * S5_ST:(m + 1) * S5_ST] = bu[:, :S5_ST]
        xs_ref[:, S_LANES + m * S5_ST:S_LANES + (m + 1) * S5_ST] = bu[:, S5_ST:]

    for lb in range(S_LANES // SCAN_LANES):
        re = slice(lb * SCAN_LANES, (lb + 1) * SCAN_LANES)
        im = slice(S_LANES + lb * SCAN_LANES, S_LANES + (lb + 1) * SCAN_LANES)
        ar = jnp.broadcast_to(ar_ref[:, re], (BATCH, SCAN_LANES))
        ai = jnp.broadcast_to(ai_ref[:, re], (BATCH, SCAN_LANES))

        def step(t, carry):
            sr, si = carry
            r = pl.ds(pl.multiple_of(t * BATCH, BATCH), BATCH)
            nr = ar * sr - ai * si + xs_ref[r, re]
            ni = ar * si + ai * sr + xs_ref[r, im]
            xs_ref[r, re] = nr
            xs_ref[r, im] = ni
            return nr, ni

        sr, si = lax.fori_loop(0, steps, step, (st_ref[:, re], st_ref[:, im]), unroll=4)
        st_ref[:, re] = sr
        st_ref[:, im] = si

    ys = []
    for m in range(S5_SPLIT):
        xr = xs_ref[:, m * S5_ST:(m + 1) * S5_ST].astype(BF16)
        xi = xs_ref[:, S_LANES + m * S5_ST:S_LANES + (m + 1) * S5_ST].astype(BF16)
        ys.append(_mm(xr, cblk_ref[m, 0:S5_ST, :]) + _mm(xi, cblk_ref[m, S5_ST:2 * S5_ST, :]))
    y = jnp.concatenate(ys, axis=1) + d_ref[...] * u
    y_bt = jnp.swapaxes(y.reshape(steps, BATCH, C_WIDTH), 0, 1).reshape(rows, C_WIDTH)
    o = _mm(jax.nn.gelu(y_bt).astype(BF16), wout_ref[...])
    out = x + o[:, :D_MODEL] * jax.nn.sigmoid(o[:, D_MODEL:])
    o_ref[...] = out.reshape(BATCH, steps, D_MODEL)


def _s5_discretize(lam_re, lam_im, log_dt, b_re, b_im, c_re, c_im):
    dt = jnp.exp(log_dt)[:, None]
    mag = jnp.exp(lam_re * dt)
    ar = mag * jnp.cos(lam_im * dt)
    ai = mag * jnp.sin(lam_im * dt)
    den = lam_re * lam_re + lam_im * lam_im
    qr = ((ar - 1.0) * lam_re + ai * lam_im) / den
    qi = (ai * lam_re - (ar - 1.0) * lam_im) / den
    bbr = qr[..., None] * b_re - qi[..., None] * b_im
    bbi = qr[..., None] * b_im + qi[..., None] * b_re
    gs = C_GROUPS // S5_SPLIT
    eye = jnp.eye(gs, dtype=F32)
    split = lambda a: a.reshape((S5_SPLIT, gs) + a.shape[1:])
    blk_b = lambda a: jnp.einsum('mgpc,gh->mgchp', split(a), eye).reshape(S5_SPLIT, S5_CH, S5_ST)
    blk_c = lambda a: jnp.einsum('mgcp,gh->mgphc', split(a), eye).reshape(S5_SPLIT, S5_ST, S5_CH)
    bblk = jnp.concatenate([blk_b(bbr), blk_b(bbi)], axis=2)
    cblk = jnp.concatenate([blk_c(c_re), -blk_c(c_im)], axis=1)
    return (bblk.astype(BF16), ar.reshape(1, S_LANES), ai.reshape(1, S_LANES), cblk.astype(BF16))


def _odd_layer(x, norm, w_in, ssm, d_skip, w_out):
    bblk, ar, ai, cblk = ssm
    spec = pl.BlockSpec((BATCH, SCAN_STEPS, D_MODEL), lambda i: (0, i, 0))
    return pl.pallas_call(
        _odd_kernel,
        grid=(SEQ // SCAN_STEPS,),
        in_specs=[
            spec,
            _resident((1, D_MODEL)),
            _resident((D_MODEL, C_WIDTH)),
            _resident((S5_SPLIT, S5_CH, 2 * S5_ST)),
            _resident((1, S_LANES)),
            _resident((1, S_LANES)),
            _resident((S5_SPLIT, 2 * S5_ST, S5_CH)),
            _resident((1, C_WIDTH)),
            _resident((C_WIDTH, 2 * D_MODEL)),
        ],
        out_specs=spec,
        out_shape=jax.ShapeDtypeStruct((BATCH, SEQ, D_MODEL), F32),
        scratch_shapes=[pltpu.VMEM((SCAN_STEPS * BATCH, 2 * S_LANES), F32),
                        pltpu.VMEM((BATCH, 2 * S_LANES), F32)],
        compiler_params=_params(1),
        name="odd_mixer",
    )(x, norm.reshape(1, -1), w_in.astype(BF16), bblk, ar, ai, cblk, d_skip.reshape(1, -1),
      w_out.astype(BF16))


def _kv_kernel(m_ref, g_ref, wk_ref, wv_ref, k_ref, v_ref):
    mn = _rms(m_ref[...], g_ref[...]).astype(BF16)
    k_ref[...] = _mm(mn, wk_ref[...]).astype(BF16)
    v_ref[...] = _mm(mn, wv_ref[...]).astype(BF16)


def _kv_proj(mem, norm, wk, wv):
    spec = pl.BlockSpec((None, MEM_LEN, D_MODEL), lambda b: (b, 0, 0))
    shape = jax.ShapeDtypeStruct((BATCH, MEM_LEN, D_MODEL), BF16)
    return pl.pallas_call(
        _kv_kernel,
        grid=(BATCH,),
        in_specs=[spec, _resident((1, D_MODEL)), _resident((D_MODEL, D_MODEL)),
                  _resident((D_MODEL, D_MODEL))],
        out_specs=[spec, spec],
        out_shape=[shape, shape],
        compiler_params=_params(1),
        name="ca_kv_proj",
    )(mem, norm.reshape(1, -1), wk.astype(BF16), wv.astype(BF16))


def _ca_kernel(x_ref, g_ref, wq_ref, k_ref, v_ref, wo_ref, o_ref):
    x = x_ref[...]
    q = _mm(_rms(x, g_ref[...]).astype(BF16), wq_ref[...]).astype(BF16)
    heads = []
    for h in range(CA_HEADS):
        hs = slice(h * CA_HEAD_DIM, (h + 1) * CA_HEAD_DIM)
        sc = lax.dot_general(q[:, hs], k_ref[:, hs], (((1,), (1,)), ((), ())),
                             preferred_element_type=F32) * (CA_HEAD_DIM ** -0.5)
        e = jnp.exp(sc - jnp.max(sc, axis=-1, keepdims=True))
        p = e / jnp.sum(e, axis=-1, keepdims=True)
        heads.append(_mm(p.astype(BF16), v_ref[:, hs]))
    o = jnp.concatenate(heads, axis=1).astype(BF16)
    o_ref[...] = x + _mm(o, wo_ref[...])


def _cross_attention(x, k, v, norm, wq, wo):
    kv_spec = pl.BlockSpec((None, MEM_LEN, D_MODEL), lambda b, s: (b, 0, 0))
    return pl.pallas_call(
        _ca_kernel,
        grid=(BATCH, SEQ // TOKEN_TILE),
        in_specs=[_tok_spec(D_MODEL), _resident((1, D_MODEL)), _resident((D_MODEL, D_MODEL)),
                  kv_spec, kv_spec, _resident((D_MODEL, D_MODEL))],
        out_specs=_tok_spec(D_MODEL),
        out_shape=jax.ShapeDtypeStruct((BATCH, SEQ, D_MODEL), F32),
        compiler_params=_params(2),
        name="cross_attention",
    )(x, norm.reshape(1, -1), wq.astype(BF16), k, v, wo.astype(BF16))


def _ffn_kernel(x_ref, g_ref, wg_ref, wu_ref, wd_ref, fg_ref, o_ref, *, final):
    x = x_ref[...]
    xn = _rms(x, g_ref[...]).astype(BF16)
    hc = FFN_HIDDEN // FFN_CHUNKS
    y = x
    for c in range(FFN_CHUNKS):
        cs = slice(c * hc, (c + 1) * hc)
        h = jax.nn.silu(_mm(xn, wg_ref[:, cs])) * _mm(xn, wu_ref[:, cs])
        y = y + _mm(h.astype(BF16), wd_ref[cs, :])
    if final:
        y = _rms(y, fg_ref[...])
    o_ref[...] = y


def _ffn(x, norm, wg, wu, wd, final_norm, final):
    return pl.pallas_call(
        functools.partial(_ffn_kernel, final=final),
        grid=(BATCH, SEQ // TOKEN_TILE),
        in_specs=[_tok_spec(D_MODEL), _resident((1, D_MODEL)),
                  _resident((D_MODEL, FFN_HIDDEN)), _resident((D_MODEL, FFN_HIDDEN)),
                  _resident((FFN_HIDDEN, D_MODEL)), _resident((1, D_MODEL))],
        out_specs=_tok_spec(D_MODEL),
        out_shape=jax.ShapeDtypeStruct((BATCH, SEQ, D_MODEL), F32),
        compiler_params=_params(2),
        name="ffn_final" if final else "ffn",
    )(x, norm.reshape(1, -1), wg.astype(BF16), wu.astype(BF16), wd.astype(BF16),
      final_norm.reshape(1, -1))


def kernel(x, mem, e_norm, e_w_in, e_gmlp_w, e_gmlp_b, e_conv_w, e_conv_b, e_conv_ln_g, e_conv_ln_b, e_w_out, o_norm, o_w_in, o_lam_re, o_lam_im, o_log_dt, o_b_re, o_b_im, o_c_re, o_c_im, o_d, o_w_out, ca_norm, ca_mem_norm, ca_wq, ca_wk, ca_wv, ca_wo, ffn_norm, ffn_w_gate, ffn_w_up, ffn_w_down, final_norm):
    for i in range(DEPTH):
        j = i // 2
        if i % 2 == 0:
            x = _even_layer(x, e_norm[j], e_w_in[j], e_gmlp_w[j], e_gmlp_b[j], e_conv_w[j],
                            e_conv_b[j], e_conv_ln_g[j], e_conv_ln_b[j], e_w_out[j])
        else:
            ssm = _s5_discretize(o_lam_re[j], o_lam_im[j], o_log_dt[j], o_b_re[j], o_b_im[j],
                                 o_c_re[j], o_c_im[j])
            x = _odd_layer(x, o_norm[j], o_w_in[j], ssm, o_d[j], o_w_out[j])
        k, v = _kv_proj(mem, ca_mem_norm[i], ca_wk[i], ca_wv[i])
        x = _cross_attention(x, k, v, ca_norm[i], ca_wq[i], ca_wo[i])
        x = _ffn(x, ffn_norm[i], ffn_w_gate[i], ffn_w_up[i], ffn_w_down[i], final_norm,
                 final=(i == DEPTH - 1))
    return x
```

```python
import functools
import math

import jax
import jax.numpy as jnp
from jax import lax
from jax.experimental import pallas as pl
from jax.experimental.pallas import tpu as pltpu

F32 = jnp.float32
BF16 = jnp.bfloat16

D_MODEL = 1024
BATCH = 8
SEQ = 4096
DEPTH = 2
CHUNK = 64
MEM_LEN = 256
EPS = 1e-6
A_WIDTH = 512
A_GROUPS = 4
A_GROUP_DIM = A_WIDTH // A_GROUPS
GMLP_BLOCK = 128
B_WIDTH = 512
CONV_WIDTH = 31
MIX_WIDTH = A_WIDTH + B_WIDTH
IN_WIDTH = 2 * A_WIDTH + 2 * B_WIDTH
C_WIDTH = 512
C_GROUP_CH = 16
C_GROUPS = C_WIDTH // C_GROUP_CH
C_STATE = 64
S_LANES = C_GROUPS * C_STATE
S5_SPLIT = 2
S5_CH = C_WIDTH // S5_SPLIT
S5_ST = S_LANES // S5_SPLIT
CA_HEADS = 4
CA_HEAD_DIM = D_MODEL // CA_HEADS
FFN_HIDDEN = -(-8 * D_MODEL // (3 * 256)) * 256

VMEM_LIMIT_BYTES = 56 * 1024 * 1024
SUBLANES = 8
LANES = 128

TOKEN_TILE = 512
CONV_HALO = 32
CONV_ROWS = 128
SCAN_STEPS = 64
SCAN_LANES = 512
MXU_TILE = 256
PROJ_SLICE = MXU_TILE
PROJ_SLICES = IN_WIDTH // PROJ_SLICE
FFN_CHUNK_TILES = (6, 5)


def _params(n_axes):
    return pltpu.CompilerParams(
        dimension_semantics=("arbitrary",) * n_axes,
        vmem_limit_bytes=VMEM_LIMIT_BYTES)


def _resident(shape):
    return pl.BlockSpec(shape, lambda *_: (0,) * len(shape),
                        pipeline_mode=pl.Buffered(1))


def _tok_spec(width):
    return pl.BlockSpec((None, TOKEN_TILE, width), lambda b, s: (b, s, 0))


def _rms(x, g):
    return x * lax.rsqrt(jnp.mean(x * x, axis=-1, keepdims=True) + EPS) * g


def _ln(x):
    mu = jnp.mean(x, axis=-1, keepdims=True)
    xc = x - mu
    return xc * lax.rsqrt(jnp.mean(xc * xc, axis=-1, keepdims=True) + EPS)


def _mm(a, b):
    return jnp.dot(a, b, preferred_element_type=F32)


def _even_kernel(xa_ref, xc_ref, g_ref, win_ref, gw_ref, gb_ref, cw_ref, cb_ref, lng_ref,
                 lnb_ref, wout_ref, o_ref, proj_ref, hn_ref, hext_ref, conv_ref):
    tm = TOKEN_TILE
    nblk = tm // GMLP_BLOCK
    i = pl.program_id(0)

    @pl.when(i == 0)
    def _():
        proj_ref[...] = jnp.zeros(proj_ref.shape, F32)
        hext_ref[:, 0:CONV_HALO, :] = jnp.zeros((B_WIDTH // LANES, CONV_HALO, LANES), F32)

    hn_ref[...] = _rms(xa_ref[...], g_ref[...]).astype(BF16)
    col = lambda lo, hi: jnp.concatenate(
        [proj_ref[s] for s in range(lo // PROJ_SLICE, hi // PROJ_SLICE)], axis=1)

    u = jax.nn.gelu(col(0, A_WIDTH))
    v = _ln(jax.nn.gelu(col(A_WIDTH, 2 * A_WIDTH))).astype(BF16)
    ri = lax.broadcasted_iota(jnp.int32, (GMLP_BLOCK, GMLP_BLOCK), 0) >> 6
    ci = lax.broadcasted_iota(jnp.int32, (GMLP_BLOCK, GMLP_BLOCK), 1) >> 6
    keep = ci <= ri
    per_group = []
    for g in range(A_GROUPS):
        gs = slice(g * A_GROUP_DIM, (g + 1) * A_GROUP_DIM)
        rhs = jnp.concatenate(
            [v[nb * GMLP_BLOCK:(nb + 1) * GMLP_BLOCK, gs] for nb in range(nblk)], axis=1)
        wm = jnp.where(keep, gw_ref[g], 0.0).astype(BF16)
        per_group.append(_mm(wm, rhs))
    sg = jnp.concatenate(
        [jnp.concatenate([per_group[g][:, nb * A_GROUP_DIM:(nb + 1) * A_GROUP_DIM]
                          for g in range(A_GROUPS)], axis=1) for nb in range(nblk)], axis=0)
    bias = jnp.concatenate([gb_ref[...]] * nblk, axis=0)
    out_a = (u * (sg + bias)).astype(BF16)

    h = col(2 * A_WIDTH, 2 * A_WIDTH + B_WIDTH) * jax.nn.sigmoid(col(2 * A_WIDTH + B_WIDTH, IN_WIDTH))
    seq_start = (i % (SEQ // tm)) == 1
    for cb in range(B_WIDTH // LANES):
        hext_ref[cb, 0:CONV_HALO, :] = jnp.where(seq_start, 0.0, hext_ref[cb, 0:CONV_HALO, :])
        hext_ref[cb, CONV_HALO:CONV_HALO + tm, :] = h[:, cb * LANES:(cb + 1) * LANES]

    blocks_per_slice = (B_WIDTH // LANES) * (tm // CONV_ROWS) // PROJ_SLICES
    row_blocks = tm // CONV_ROWS

    def conv_and_project(s, carry):
        proj_ref[s] = _mm(hn_ref[...], win_ref[s])
        for j in range(blocks_per_slice):
            blk = s * blocks_per_slice + j
            cb = blk // row_blocks
            r0 = pl.multiple_of((blk % row_blocks) * CONV_ROWS, CONV_ROWS)
            win = hext_ref[cb, pl.ds(r0, CONV_ROWS + CONV_HALO), :]
            acc = jnp.zeros((CONV_ROWS, LANES), F32)
            for r in range(SUBLANES):
                shifted = win if r == 0 else pltpu.roll(win, r, axis=0)
                for q in range(CONV_HALO // SUBLANES):
                    lag = SUBLANES * q + r
                    if lag < CONV_WIDTH:
                        k = CONV_WIDTH - 1 - lag
                        lo = CONV_HALO - SUBLANES * q
                        acc = acc + cw_ref[cb, k:k + 1, :] * shifted[lo:lo + CONV_ROWS, :]
            conv_ref[cb, pl.ds(r0, CONV_ROWS), :] = acc
        return carry

    lax.fori_loop(0, PROJ_SLICES, conv_and_project, 0)
    for cb in range(B_WIDTH // LANES):
        hext_ref[cb, 0:CONV_HALO, :] = hext_ref[cb, tm:tm + CONV_HALO, :]
    conv = jnp.concatenate([conv_ref[cb] for cb in range(B_WIDTH // LANES)], axis=1)
    c = _ln(conv + cb_ref[...]) * lng_ref[...] + lnb_ref[...]
    out_b = jax.nn.silu(c).astype(BF16)

    mix = _mm(out_a, wout_ref[0:A_WIDTH, :]) + _mm(out_b, wout_ref[A_WIDTH:MIX_WIDTH, :])
    o_ref[...] = xc_ref[...] + mix


def _even_layer(x, norm, w_in, gmlp_w, gmlp_b, conv_w, conv_b, ln_g, ln_b, w_out):
    gb_full = jnp.repeat(gmlp_b.T, A_GROUP_DIM, axis=1)
    row = lambda a: a.reshape(1, -1)
    n_tiles = BATCH * SEQ // TOKEN_TILE
    lanes_blocks = B_WIDTH // LANES
    lead = pl.BlockSpec((TOKEN_TILE, D_MODEL), lambda i: (jnp.minimum(i, n_tiles - 1), 0))
    lag1 = pl.BlockSpec((TOKEN_TILE, D_MODEL), lambda i: (jnp.maximum(i - 1, 0), 0))
    x2 = x.reshape(BATCH * SEQ, D_MODEL)
    w_in_s = w_in.astype(BF16).reshape(D_MODEL, PROJ_SLICES, PROJ_SLICE).transpose(1, 0, 2)
    conv_w_b = conv_w.reshape(CONV_WIDTH, lanes_blocks, LANES).transpose(1, 0, 2)
    out = pl.pallas_call(
        _even_kernel,
        grid=(n_tiles + 1,),
        in_specs=[
            lead,
            lag1,
            _resident((1, D_MODEL)),
            _resident((PROJ_SLICES, D_MODEL, PROJ_SLICE)),
            _resident((A_GROUPS, GMLP_BLOCK, GMLP_BLOCK)),
            _resident((GMLP_BLOCK, A_WIDTH)),
            _resident((lanes_blocks, CONV_WIDTH, LANES)),
            _resident((1, B_WIDTH)),
            _resident((1, B_WIDTH)),
            _resident((1, B_WIDTH)),
            _resident((MIX_WIDTH, D_MODEL)),
        ],
        out_specs=lag1,
        out_shape=jax.ShapeDtypeStruct((BATCH * SEQ, D_MODEL), F32),
        scratch_shapes=[pltpu.VMEM((PROJ_SLICES, TOKEN_TILE, PROJ_SLICE), F32),
                        pltpu.VMEM((TOKEN_TILE, D_MODEL), BF16),
                        pltpu.VMEM((lanes_blocks, CONV_HALO + TOKEN_TILE, LANES), F32),
                        pltpu.VMEM((lanes_blocks, TOKEN_TILE, LANES), F32)],
        compiler_params=_params(1),
        name="even_mixer",
    )(x2, x2, row(norm), w_in_s, gmlp_w, gb_full, conv_w_b, row(conv_b), row(ln_g),
      row(ln_b), w_out.astype(BF16))
    return out.reshape(BATCH, SEQ, D_MODEL)


def _odd_kernel(x_ref, g_ref, win_ref, bblk_ref, ar_ref, ai_ref, cblk_ref, d_ref, wout_ref,
                o_ref, xs_ref, st_ref):
    steps = SCAN_STEPS
    rows = steps * BATCH

    @pl.when(pl.program_id(0) == 0)
    def _():
        st_ref[...] = jnp.zeros(st_ref.shape, F32)

    x = x_ref[...].reshape(rows, D_MODEL)
    u_bt = _mm(_rms(x, g_ref[...]).astype(BF16), win_ref[...])
    u = jnp.swapaxes(u_bt.reshape(BATCH, steps, C_WIDTH), 0, 1).reshape(rows, C_WIDTH)
    ub = u.astype(BF16)
    for m in range(S5_SPLIT):
        bu = _mm(ub[:, m * S5_CH:(m + 1) * S5_CH], bblk_ref[m])
        xs_ref[:, m * S5_ST:(m + 1) * S5_ST] = bu[:, :S5_ST]
        xs_ref[:, S_LANES + m * S5_ST:S_LANES + (m + 1) * S5_ST] = bu[:, S5_ST:]

    for lb in range(S_LANES // SCAN_LANES):
        re = slice(lb * SCAN_LANES, (lb + 1) * SCAN_LANES)
        im = slice(S_LANES + lb * SCAN_LANES, S_LANES + (lb + 1) * SCAN_LANES)
        ar = jnp.broadcast_to(ar_ref[:, re], (BATCH, SCAN_LANES))
        ai = jnp.broadcast_to(ai_ref[:, re], (BATCH, SCAN_LANES))

        def step(t, carry):
            sr, si = carry
            r = pl.ds(pl.multiple_of(t * BATCH, BATCH), BATCH)
            nr = ar * sr - ai * si + xs_ref[r, re]
            ni = ar * si + ai * sr + xs_ref[r, im]
            xs_ref[r, re] = nr
            xs_ref[r, im] = ni
            return nr, ni

        sr, si = lax.fori_loop(0, steps, step, (st_ref[:, re], st_ref[:, im]), unroll=4)
        st_ref[:, re] = sr
        st_ref[:, im] = si

    ys = []
    for m in range(S5_SPLIT):
        xr = xs_ref[:, m * S5_ST:(m + 1) * S5_ST].astype(BF16)
        xi = xs_ref[:, S_LANES + m * S5_ST:S_LANES + (m + 1) * S5_ST].astype(BF16)
        ys.append(_mm(xr, cblk_ref[m, 0:S5_ST, :]) + _mm(xi, cblk_ref[m, S5_ST:2 * S5_ST, :]))
    y = jnp.concatenate(ys, axis=1) + d_ref[...] * u
    y_bt = jnp.swapaxes(y.reshape(steps, BATCH, C_WIDTH), 0, 1).reshape(rows, C_WIDTH)
    o = _mm(jax.nn.gelu(y_bt).astype(BF16), wout_ref[...])
    out = x + o[:, :D_MODEL] * jax.nn.sigmoid(o[:, D_MODEL:])
    o_ref[...] = out.reshape(BATCH, steps, D_MODEL)


def _s5_discretize(lam_re, lam_im, log_dt, b_re, b_im, c_re, c_im):
    dt = jnp.exp(log_dt)[:, None]
    mag = jnp.exp(lam_re * dt)
    ar = mag * jnp.cos(lam_im * dt)
    ai = mag * jnp.sin(lam_im * dt)
    den = lam_re * lam_re + lam_im * lam_im
    qr = ((ar - 1.0) * lam_re + ai * lam_im) / den
    qi = (ai * lam_re - (ar - 1.0) * lam_im) / den
    bbr = qr[..., None] * b_re - qi[..., None] * b_im
    bbi = qr[..., None] * b_im + qi[..., None] * b_re
    gs = C_GROUPS // S5_SPLIT
    eye = jnp.eye(gs, dtype=F32)
    split = lambda a: a.reshape((S5_SPLIT, gs) + a.shape[1:])
    blk_b = lambda a: jnp.einsum('mgpc,gh->mgchp', split(a), eye).reshape(S5_SPLIT, S5_CH, S5_ST)
    blk_c = lambda a: jnp.einsum('mgcp,gh->mgphc', split(a), eye).reshape(S5_SPLIT, S5_ST, S5_CH)
    bblk = jnp.concatenate([blk_b(bbr), blk_b(bbi)], axis=2)
    cblk = jnp.concatenate([blk_c(c_re), -blk_c(c_im)], axis=1)
    return (bblk.astype(BF16), ar.reshape(1, S_LANES), ai.reshape(1, S_LANES), cblk.astype(BF16))


def _odd_layer(x, norm, w_in, ssm, d_skip, w_out):
    bblk, ar, ai, cblk = ssm
    spec = pl.BlockSpec((BATCH, SCAN_STEPS, D_MODEL), lambda i: (0, i, 0))
    return pl.pallas_call(
        _odd_kernel,
        grid=(SEQ // SCAN_STEPS,),
        in_specs=[
            spec,
            _resident((1, D_MODEL)),
            _resident((D_MODEL, C_WIDTH)),
            _resident((S5_SPLIT, S5_CH, 2 * S5_ST)),
            _resident((1, S_LANES)),
            _resident((1, S_LANES)),
            _resident((S5_SPLIT, 2 * S5_ST, S5_CH)),
            _resident((1, C_WIDTH)),
            _resident((C_WIDTH, 2 * D_MODEL)),
        ],
        out_specs=spec,
        out_shape=jax.ShapeDtypeStruct((BATCH, SEQ, D_MODEL), F32),
        scratch_shapes=[pltpu.VMEM((SCAN_STEPS * BATCH, 2 * S_LANES), F32),
                        pltpu.VMEM((BATCH, 2 * S_LANES), F32)],
        compiler_params=_params(1),
        name="odd_mixer",
    )(x, norm.reshape(1, -1), w_in.astype(BF16), bblk, ar, ai, cblk, d_skip.reshape(1, -1),
      w_out.astype(BF16))


def _kv_kernel(m_ref, g_ref, wk_ref, wv_ref, k_ref, v_ref):
    mn = _rms(m_ref[...], g_ref[...]).astype(BF16)
    k_ref[...] = _mm(mn, wk_ref[...]).astype(BF16)
    v_ref[...] = _mm(mn, wv_ref[...]).astype(BF16)


def _kv_proj(mem, norm, wk, wv):
    spec = pl.BlockSpec((None, MEM_LEN, D_MODEL), lambda b: (b, 0, 0))
    shape = jax.ShapeDtypeStruct((BATCH, MEM_LEN, D_MODEL), BF16)
    return pl.pallas_call(
        _kv_kernel,
        grid=(BATCH,),
        in_specs=[spec, _resident((1, D_MODEL)), _resident((D_MODEL, D_MODEL)),
                  _resident((D_MODEL, D_MODEL))],
        out_specs=[spec, spec],
        out_shape=[shape, shape],
        compiler_params=_params(1),
        name="ca_kv_proj",
    )(mem, norm.reshape(1, -1), wk.astype(BF16), wv.astype(BF16))


def _ca_kernel(x_ref, g_ref, wq_ref, k_ref, v_ref, wo_ref, o_ref):
    x = x_ref[...]
    q = _mm(_rms(x, g_ref[...]).astype(BF16), wq_ref[...]).astype(BF16)
    heads = []
    for h in range(CA_HEADS):
        hs = slice(h * CA_HEAD_DIM, (h + 1) * CA_HEAD_DIM)
        sc = lax.dot_general(q[:, hs], k_ref[:, hs], (((1,), (1,)), ((), ())),
                             preferred_element_type=F32) * (CA_HEAD_DIM ** -0.5)
        e = jnp.exp(sc - jnp.max(sc, axis=-1, keepdims=True))
        p = e / jnp.sum(e, axis=-1, keepdims=True)
        heads.append(_mm(p.astype(BF16), v_ref[:, hs]))
    o = jnp.concatenate(heads, axis=1).astype(BF16)
    o_ref[...] = x + _mm(o, wo_ref[...])


def _cross_attention(x, k, v, norm, wq, wo):
    kv_spec = pl.BlockSpec((None, MEM_LEN, D_MODEL), lambda b, s: (b, 0, 0))
    return pl.pallas_call(
        _ca_kernel,
        grid=(BATCH, SEQ // TOKEN_TILE),
        in_specs=[_tok_spec(D_MODEL), _resident((1, D_MODEL)), _resident((D_MODEL, D_MODEL)),
                  kv_spec, kv_spec, _resident((D_MODEL, D_MODEL))],
        out_specs=_tok_spec(D_MODEL),
        out_shape=jax.ShapeDtypeStruct((BATCH, SEQ, D_MODEL), F32),
        compiler_params=_params(2),
        name="cross_attention",
    )(x, norm.reshape(1, -1), wq.astype(BF16), k, v, wo.astype(BF16))


def _ffn_kernel(x_ref, g_ref, wg_ref, wu_ref, wd_ref, fg_ref, o_ref, *, final):
    x = x_ref[...]
    xn = _rms(x, g_ref[...]).astype(BF16)
    y = x
    lo = 0
    for tiles in FFN_CHUNK_TILES:
        cs = slice(lo, lo + tiles * MXU_TILE)
        lo += tiles * MXU_TILE
        h = jax.nn.silu(_mm(xn, wg_ref[:, cs])) * _mm(xn, wu_ref[:, cs])
        y = y + _mm(h.astype(BF16), wd_ref[cs, :])
    if final:
        y = _rms(y, fg_ref[...])
    o_ref[...] = y


def _ffn(x, norm, wg, wu, wd, final_norm, final):
    return pl.pallas_call(
        functools.partial(_ffn_kernel, final=final),
        grid=(BATCH, SEQ // TOKEN_TILE),
        in_specs=[_tok_spec(D_MODEL), _resident((1, D_MODEL)),
                  _resident((D_MODEL, FFN_HIDDEN)), _resident((D_MODEL, FFN_HIDDEN)),
                  _resident((FFN_HIDDEN, D_MODEL)), _resident((1, D_MODEL))],
        out_specs=_tok_spec(D_MODEL),
        out_shape=jax.ShapeDtypeStruct((BATCH, SEQ, D_MODEL), F32),
        compiler_params=_params(2),
        name="ffn_final" if final else "ffn",
    )(x, norm.reshape(1, -1), wg.astype(BF16), wu.astype(BF16), wd.astype(BF16),
      final_norm.reshape(1, -1))


def kernel(x, mem, e_norm, e_w_in, e_gmlp_w, e_gmlp_b, e_conv_w, e_conv_b, e_conv_ln_g, e_conv_ln_b, e_w_out, o_norm, o_w_in, o_lam_re, o_lam_im, o_log_dt, o_b_re, o_b_im, o_c_re, o_c_im, o_d, o_w_out, ca_norm, ca_mem_norm, ca_wq, ca_wk, ca_wv, ca_wo, ffn_norm, ffn_w_gate, ffn_w_up, ffn_w_down, final_norm):
    for i in range(DEPTH):
        j = i // 2
        if i % 2 == 0:
            x = _even_layer(x, e_norm[j], e_w_in[j], e_gmlp_w[j], e_gmlp_b[j], e_conv_w[j],
                            e_conv_b[j], e_conv_ln_g[j], e_conv_ln_b[j], e_w_out[j])
        else:
            ssm = _s5_discretize(o_lam_re[j], o_lam_im[j], o_log_dt[j], o_b_re[j], o_b_im[j],
                                 o_c_re[j], o_c_im[j])
            x = _odd_layer(x, o_norm[j], o_w_in[j], ssm, o_d[j], o_w_out[j])
        k, v = _kv_proj(mem, ca_mem_norm[i], ca_wk[i], ca_wv[i])
        x = _cross_attention(x, k, v, ca_norm[i], ca_wq[i], ca_wo[i])
        x = _ffn(x, ffn_norm[i], ffn_w_gate[i], ffn_w_up[i], ffn_w_down[i], final_norm,
                 final=(i == DEPTH - 1))
    return x
```

```python
import functools
import math

import jax
import jax.numpy as jnp
from jax import lax
from jax.experimental import pallas as pl
from jax.experimental.pallas import tpu as pltpu

F32 = jnp.float32
BF16 = jnp.bfloat16

D_MODEL = 1024
BATCH = 8
SEQ = 4096
DEPTH = 2
CHUNK = 64
MEM_LEN = 256
EPS = 1e-6
A_WIDTH = 512
A_GROUPS = 4
A_GROUP_DIM = A_WIDTH // A_GROUPS
GMLP_BLOCK = 128
B_WIDTH = 512
CONV_WIDTH = 31
MIX_WIDTH = A_WIDTH + B_WIDTH
IN_WIDTH = 2 * A_WIDTH + 2 * B_WIDTH
C_WIDTH = 512
C_GROUP_CH = 16
C_GROUPS = C_WIDTH // C_GROUP_CH
C_STATE = 64
S_LANES = C_GROUPS * C_STATE
S5_SPLIT = 2
S5_CH = C_WIDTH // S5_SPLIT
S5_ST = S_LANES // S5_SPLIT
CA_HEADS = 4
CA_HEAD_DIM = D_MODEL // CA_HEADS
FFN_HIDDEN = -(-8 * D_MODEL // (3 * 256)) * 256

VMEM_LIMIT_BYTES = 56 * 1024 * 1024
SUBLANES = 8
LANES = 128
MXU_TILE = 256

EVEN_TILE = 512
CA_TILE = 1024
FFN_TILE = 1024
CONV_HALO = 32
CONV_ROWS = 128
SCAN_STEPS = 64
SCAN_LANES = 1024
SCAN_UNROLL = 4
FFN_CHUNK_TILES = (4, 4, 3)


def _params(n_axes):
    return pltpu.CompilerParams(
        dimension_semantics=("arbitrary",) * n_axes,
        vmem_limit_bytes=VMEM_LIMIT_BYTES)


def _resident(shape):
    return pl.BlockSpec(shape, lambda *_: (0,) * len(shape),
                        pipeline_mode=pl.Buffered(1))


def _tok_spec(tile):
    return pl.BlockSpec((None, tile, D_MODEL), lambda b, s: (b, s, 0))


def _rms(x, g):
    return x * lax.rsqrt(jnp.mean(x * x, axis=-1, keepdims=True) + EPS) * g


def _ln(x):
    mu = jnp.mean(x, axis=-1, keepdims=True)
    xc = x - mu
    return xc * lax.rsqrt(jnp.mean(xc * xc, axis=-1, keepdims=True) + EPS)


def _mm(a, b):
    return jnp.dot(a, b, preferred_element_type=F32)


def _even_kernel(x_ref, g_ref, win_ref, gw_ref, gb_ref, cw_ref, cb_ref, lng_ref,
                 lnb_ref, wout_ref, o_ref, hext_ref, conv_ref):
    tm = EVEN_TILE
    nblk = tm // GMLP_BLOCK
    x = x_ref[...]
    hn = _rms(x, g_ref[...]).astype(BF16)
    proj = _mm(hn, win_ref[...])

    u = jax.nn.gelu(proj[:, :A_WIDTH])
    v = _ln(jax.nn.gelu(proj[:, A_WIDTH:2 * A_WIDTH])).astype(BF16)
    ri = lax.broadcasted_iota(jnp.int32, (GMLP_BLOCK, GMLP_BLOCK), 0) >> 6
    ci = lax.broadcasted_iota(jnp.int32, (GMLP_BLOCK, GMLP_BLOCK), 1) >> 6
    keep = ci <= ri
    per_group = []
    for g in range(A_GROUPS):
        gs = slice(g * A_GROUP_DIM, (g + 1) * A_GROUP_DIM)
        rhs = jnp.concatenate(
            [v[nb * GMLP_BLOCK:(nb + 1) * GMLP_BLOCK, gs] for nb in range(nblk)], axis=1)
        wm = jnp.where(keep, gw_ref[g], 0.0).astype(BF16)
        per_group.append(_mm(wm, rhs))
    sg = jnp.concatenate(
        [jnp.concatenate([per_group[g][:, nb * A_GROUP_DIM:(nb + 1) * A_GROUP_DIM]
                          for g in range(A_GROUPS)], axis=1) for nb in range(nblk)], axis=0)
    bias = jnp.concatenate([gb_ref[...]] * nblk, axis=0)
    out_a = u * (sg + bias)

    h = proj[:, 2 * A_WIDTH:2 * A_WIDTH + B_WIDTH] * jax.nn.sigmoid(proj[:, 2 * A_WIDTH + B_WIDTH:])

    @pl.when(pl.program_id(1) == 0)
    def _():
        hext_ref[0:CONV_HALO, :] = jnp.zeros((CONV_HALO, B_WIDTH), F32)

    hext_ref[CONV_HALO:CONV_HALO + tm, :] = h
    for cb in range(B_WIDTH // LANES):
        cs = slice(cb * LANES, (cb + 1) * LANES)
        for rc in range(tm // CONV_ROWS):
            r0 = rc * CONV_ROWS
            win = hext_ref[r0:r0 + CONV_ROWS + CONV_HALO, cs]
            acc = jnp.zeros((CONV_ROWS, LANES), F32)
            for r in range(SUBLANES):
                shifted = win if r == 0 else pltpu.roll(win, r, axis=0)
                for q in range(CONV_HALO // SUBLANES):
                    lag = SUBLANES * q + r
                    if lag < CONV_WIDTH:
                        k = CONV_WIDTH - 1 - lag
                        lo = CONV_HALO - SUBLANES * q
                        acc = acc + cw_ref[k:k + 1, cs] * shifted[lo:lo + CONV_ROWS, :]
            conv_ref[r0:r0 + CONV_ROWS, cs] = acc
    hext_ref[0:CONV_HALO, :] = hext_ref[tm:tm + CONV_HALO, :]
    c = _ln(conv_ref[...] + cb_ref[...]) * lng_ref[...] + lnb_ref[...]
    out_b = jax.nn.silu(c)

    mix = (_mm(out_a.astype(BF16), wout_ref[0:A_WIDTH, :])
           + _mm(out_b.astype(BF16), wout_ref[A_WIDTH:MIX_WIDTH, :]))
    o_ref[...] = x + mix


def _even_layer(x, norm, w_in, gmlp_w, gmlp_b, conv_w, conv_b, ln_g, ln_b, w_out):
    gb_full = jnp.repeat(gmlp_b.T, A_GROUP_DIM, axis=1)
    row = lambda a: a.reshape(1, -1)
    return pl.pallas_call(
        _even_kernel,
        grid=(BATCH, SEQ // EVEN_TILE),
        in_specs=[
            _tok_spec(EVEN_TILE),
            _resident((1, D_MODEL)),
            _resident((D_MODEL, IN_WIDTH)),
            _resident((A_GROUPS, GMLP_BLOCK, GMLP_BLOCK)),
            _resident((GMLP_BLOCK, A_WIDTH)),
            _resident((CONV_WIDTH, B_WIDTH)),
            _resident((1, B_WIDTH)),
            _resident((1, B_WIDTH)),
            _resident((1, B_WIDTH)),
            _resident((MIX_WIDTH, D_MODEL)),
        ],
        out_specs=_tok_spec(EVEN_TILE),
        out_shape=jax.ShapeDtypeStruct((BATCH, SEQ, D_MODEL), F32),
        scratch_shapes=[pltpu.VMEM((CONV_HALO + EVEN_TILE, B_WIDTH), F32),
                        pltpu.VMEM((EVEN_TILE, B_WIDTH), F32)],
        compiler_params=_params(2),
        name="even_mixer",
    )(x, row(norm), w_in.astype(BF16), gmlp_w, gb_full, conv_w, row(conv_b), row(ln_g),
      row(ln_b), w_out.astype(BF16))


def _odd_kernel(x_ref, g_ref, win_ref, bblk_ref, ar_ref, ai_ref, cblk_ref, d_ref, wout_ref,
                o_ref, xs_ref, st_ref):
    steps = SCAN_STEPS
    rows = steps * BATCH

    @pl.when(pl.program_id(0) == 0)
    def _():
        st_ref[...] = jnp.zeros(st_ref.shape, F32)

    x = x_ref[...].reshape(rows, D_MODEL)
    u_bt = _mm(_rms(x, g_ref[...]).astype(BF16), win_ref[...])
    u = jnp.swapaxes(u_bt.reshape(BATCH, steps, C_WIDTH), 0, 1).reshape(rows, C_WIDTH)
    ub = u.astype(BF16)
    for m in range(S5_SPLIT):
        bu = _mm(ub[:, m * S5_CH:(m + 1) * S5_CH], bblk_ref[m])
        xs_ref[:, m * S5_ST:(m + 1) * S5_ST] = bu[:, :S5_ST]
        xs_ref[:, S_LANES + m * S5_ST:S_LANES + (m + 1) * S5_ST] = bu[:, S5_ST:]

    for lb in range(S_LANES // SCAN_LANES):
        re = slice(lb * SCAN_LANES, (lb + 1) * SCAN_LANES)
        im = slice(S_LANES + lb * SCAN_LANES, S_LANES + (lb + 1) * SCAN_LANES)
        ar = jnp.broadcast_to(ar_ref[:, re], (BATCH, SCAN_LANES))
        ai = jnp.broadcast_to(ai_ref[:, re], (BATCH, SCAN_LANES))

        def step(t, carry):
            sr, si = carry
            r = pl.ds(pl.multiple_of(t * BATCH, BATCH), BATCH)
            nr = ar * sr - ai * si + xs_ref[r, re]
            ni = ar * si + ai * sr + xs_ref[r, im]
            xs_ref[r, re] = nr
            xs_ref[r, im] = ni
            return nr, ni

        sr, si = lax.fori_loop(0, steps, step, (st_ref[:, re], st_ref[:, im]),
                               unroll=SCAN_UNROLL)
        st_ref[:, re] = sr
        st_ref[:, im] = si

    ys = []
    for m in range(S5_SPLIT):
        xr = xs_ref[:, m * S5_ST:(m + 1) * S5_ST].astype(BF16)
        xi = xs_ref[:, S_LANES + m * S5_ST:S_LANES + (m + 1) * S5_ST].astype(BF16)
        ys.append(_mm(xr, cblk_ref[m, 0:S5_ST, :]) + _mm(xi, cblk_ref[m, S5_ST:2 * S5_ST, :]))
    y = jnp.concatenate(ys, axis=1) + d_ref[...] * u
    y_bt = jnp.swapaxes(y.reshape(steps, BATCH, C_WIDTH), 0, 1).reshape(rows, C_WIDTH)
    o = _mm(jax.nn.gelu(y_bt).astype(BF16), wout_ref[...])
    out = x + o[:, :D_MODEL] * jax.nn.sigmoid(o[:, D_MODEL:])
    o_ref[...] = out.reshape(BATCH, steps, D_MODEL)


def _s5_discretize(lam_re, lam_im, log_dt, b_re, b_im, c_re, c_im):
    dt = jnp.exp(log_dt)[:, None]
    mag = jnp.exp(lam_re * dt)
    ar = mag * jnp.cos(lam_im * dt)
    ai = mag * jnp.sin(lam_im * dt)
    den = lam_re * lam_re + lam_im * lam_im
    qr = ((ar - 1.0) * lam_re + ai * lam_im) / den
    qi = (ai * lam_re - (ar - 1.0) * lam_im) / den
    bbr = qr[..., None] * b_re - qi[..., None] * b_im
    bbi = qr[..., None] * b_im + qi[..., None] * b_re
    gs = C_GROUPS // S5_SPLIT
    eye = jnp.eye(gs, dtype=F32)
    split = lambda a: a.reshape((S5_SPLIT, gs) + a.shape[1:])
    blk_b = lambda a: jnp.einsum('mgpc,gh->mgchp', split(a), eye).reshape(S5_SPLIT, S5_CH, S5_ST)
    blk_c = lambda a: jnp.einsum('mgcp,gh->mgphc', split(a), eye).reshape(S5_SPLIT, S5_ST, S5_CH)
    bblk = jnp.concatenate([blk_b(bbr), blk_b(bbi)], axis=2)
    cblk = jnp.concatenate([blk_c(c_re), -blk_c(c_im)], axis=1)
    return (bblk.astype(BF16), ar.reshape(1, S_LANES), ai.reshape(1, S_LANES), cblk.astype(BF16))


def _odd_layer(x, norm, w_in, ssm, d_skip, w_out):
    bblk, ar, ai, cblk = ssm
    spec = pl.BlockSpec((BATCH, SCAN_STEPS, D_MODEL), lambda i: (0, i, 0))
    return pl.pallas_call(
        _odd_kernel,
        grid=(SEQ // SCAN_STEPS,),
        in_specs=[
            spec,
            _resident((1, D_MODEL)),
            _resident((D_MODEL, C_WIDTH)),
            _resident((S5_SPLIT, S5_CH, 2 * S5_ST)),
            _resident((1, S_LANES)),
            _resident((1, S_LANES)),
            _resident((S5_SPLIT, 2 * S5_ST, S5_CH)),
            _resident((1, C_WIDTH)),
            _resident((C_WIDTH, 2 * D_MODEL)),
        ],
        out_specs=spec,
        out_shape=jax.ShapeDtypeStruct((BATCH, SEQ, D_MODEL), F32),
        scratch_shapes=[pltpu.VMEM((SCAN_STEPS * BATCH, 2 * S_LANES), F32),
                        pltpu.VMEM((BATCH, 2 * S_LANES), F32)],
        compiler_params=_params(1),
        name="odd_mixer",
    )(x, norm.reshape(1, -1), w_in.astype(BF16), bblk, ar, ai, cblk, d_skip.reshape(1, -1),
      w_out.astype(BF16))


def _kv_kernel(m_ref, g_ref, wk_ref, wv_ref, kt_ref, v_ref):
    mn = _rms(m_ref[...], g_ref[...]).astype(BF16)
    kt_ref[...] = _mm(mn, wk_ref[...]).T.astype(BF16)
    v_ref[...] = _mm(mn, wv_ref[...]).astype(BF16)


def _kv_proj(mem, norm, wk, wv):
    spec = pl.BlockSpec((None, MEM_LEN, D_MODEL), lambda b: (b, 0, 0))
    spec_t = pl.BlockSpec((None, D_MODEL, MEM_LEN), lambda b: (b, 0, 0))
    return pl.pallas_call(
        _kv_kernel,
        grid=(BATCH,),
        in_specs=[spec, _resident((1, D_MODEL)), _resident((D_MODEL, D_MODEL)),
                  _resident((D_MODEL, D_MODEL))],
        out_specs=[spec_t, spec],
        out_shape=[jax.ShapeDtypeStruct((BATCH, D_MODEL, MEM_LEN), BF16),
                   jax.ShapeDtypeStruct((BATCH, MEM_LEN, D_MODEL), BF16)],
        compiler_params=_params(1),
        name="ca_kv_proj",
    )(mem, norm.reshape(1, -1), wk.astype(BF16), wv.astype(BF16))


def _ca_kernel(x_ref, g_ref, wq_ref, kt_ref, v_ref, wo_ref, o_ref):
    x = x_ref[...]
    q = _mm(_rms(x, g_ref[...]).astype(BF16), wq_ref[...]).astype(BF16)
    heads = []
    for h in range(CA_HEADS):
        hs = slice(h * CA_HEAD_DIM, (h + 1) * CA_HEAD_DIM)
        sc = _mm(q[:, hs], kt_ref[hs, :]) * (CA_HEAD_DIM ** -0.5)
        e = jnp.exp(sc - jnp.max(sc, axis=-1, keepdims=True))
        p = e / jnp.sum(e, axis=-1, keepdims=True)
        heads.append(_mm(p.astype(BF16), v_ref[:, hs]))
    o = jnp.concatenate(heads, axis=1).astype(BF16)
    o_ref[...] = x + _mm(o, wo_ref[...])


def _cross_attention(x, kt, v, norm, wq, wo):
    kt_spec = pl.BlockSpec((None, D_MODEL, MEM_LEN), lambda b, s: (b, 0, 0))
    v_spec = pl.BlockSpec((None, MEM_LEN, D_MODEL), lambda b, s: (b, 0, 0))
    return pl.pallas_call(
        _ca_kernel,
        grid=(BATCH, SEQ // CA_TILE),
        in_specs=[_tok_spec(CA_TILE), _resident((1, D_MODEL)), _resident((D_MODEL, D_MODEL)),
                  kt_spec, v_spec, _resident((D_MODEL, D_MODEL))],
        out_specs=_tok_spec(CA_TILE),
        out_shape=jax.ShapeDtypeStruct((BATCH, SEQ, D_MODEL), F32),
        compiler_params=_params(2),
        name="cross_attention",
    )(x, norm.reshape(1, -1), wq.astype(BF16), kt, v, wo.astype(BF16))


def _ffn_kernel(x_ref, g_ref, wg_ref, wu_ref, wd_ref, fg_ref, o_ref, *, final):
    x = x_ref[...]
    xn = _rms(x, g_ref[...]).astype(BF16)
    y = x
    lo = 0
    for tiles in FFN_CHUNK_TILES:
        cs = slice(lo, lo + tiles * MXU_TILE)
        lo += tiles * MXU_TILE
        h = jax.nn.silu(_mm(xn, wg_ref[:, cs])) * _mm(xn, wu_ref[:, cs])
        y = y + _mm(h.astype(BF16), wd_ref[cs, :])
    if final:
        y = _rms(y, fg_ref[...])
    o_ref[...] = y


def _ffn(x, norm, wg, wu, wd, final_norm, final):
    return pl.pallas_call(
        functools.partial(_ffn_kernel, final=final),
        grid=(BATCH, SEQ // FFN_TILE),
        in_specs=[_tok_spec(FFN_TILE), _resident((1, D_MODEL)),
                  _resident((D_MODEL, FFN_HIDDEN)), _resident((D_MODEL, FFN_HIDDEN)),
                  _resident((FFN_HIDDEN, D_MODEL)), _resident((1, D_MODEL))],
        out_specs=_tok_spec(FFN_TILE),
        out_shape=jax.ShapeDtypeStruct((BATCH, SEQ, D_MODEL), F32),
        compiler_params=_params(2),
        name="ffn_final" if final else "ffn",
    )(x, norm.reshape(1, -1), wg.astype(BF16), wu.astype(BF16), wd.astype(BF16),
      final_norm.reshape(1, -1))


def kernel(x, mem, e_norm, e_w_in, e_gmlp_w, e_gmlp_b, e_conv_w, e_conv_b, e_conv_ln_g, e_conv_ln_b, e_w_out, o_norm, o_w_in, o_lam_re, o_lam_im, o_log_dt, o_b_re, o_b_im, o_c_re, o_c_im, o_d, o_w_out, ca_norm, ca_mem_norm, ca_wq, ca_wk, ca_wv, ca_wo, ffn_norm, ffn_w_gate, ffn_w_up, ffn_w_down, final_norm):
    for i in range(DEPTH):
        j = i // 2
        if i % 2 == 0:
            x = _even_layer(x, e_norm[j], e_w_in[j], e_gmlp_w[j], e_gmlp_b[j], e_conv_w[j],
                            e_conv_b[j], e_conv_ln_g[j], e_conv_ln_b[j], e_w_out[j])
        else:
            ssm = _s5_discretize(o_lam_re[j], o_lam_im[j], o_log_dt[j], o_b_re[j], o_b_im[j],
                                 o_c_re[j], o_c_im[j])
            x = _odd_layer(x, o_norm[j], o_w_in[j], ssm, o_d[j], o_w_out[j])
        kt, v = _kv_proj(mem, ca_mem_norm[i], ca_wk[i], ca_wv[i])
        x = _cross_attention(x, kt, v, ca_norm[i], ca_wq[i], ca_wo[i])
        x = _ffn(x, ffn_norm[i], ffn_w_gate[i], ffn_w_up[i], ffn_w_down[i], final_norm,
                 final=(i == DEPTH - 1))
    return x
```

```python
import functools
import math

import jax
import jax.numpy as jnp
from jax import lax
from jax.experimental import pallas as pl
from jax.experimental.pallas import tpu as pltpu

F32 = jnp.float32
BF16 = jnp.bfloat16

D_MODEL = 1024
BATCH = 8
SEQ = 4096
DEPTH = 2
CHUNK = 64
MEM_LEN = 256
EPS = 1e-6
A_WIDTH = 512
A_GROUPS = 4
A_GROUP_DIM = A_WIDTH // A_GROUPS
GMLP_BLOCK = 128
B_WIDTH = 512
CONV_WIDTH = 31
MIX_WIDTH = A_WIDTH + B_WIDTH
IN_WIDTH = 2 * A_WIDTH + 2 * B_WIDTH
C_WIDTH = 512
C_GROUP_CH = 16
C_GROUPS = C_WIDTH // C_GROUP_CH
C_STATE = 64
S_LANES = C_GROUPS * C_STATE
S5_SPLIT = 2
S5_CH = C_WIDTH // S5_SPLIT
S5_ST = S_LANES // S5_SPLIT
CA_HEADS = 4
CA_HEAD_DIM = D_MODEL // CA_HEADS
FFN_HIDDEN = -(-8 * D_MODEL // (3 * 256)) * 256

VMEM_LIMIT_BYTES = 56 * 1024 * 1024
SUBLANES = 8
LANES = 128
MXU_TILE = 256

EVEN_TILE = 512
CA_TILE = 1024
FFN_TILE = 1024
CONV_HALO = 32
CONV_ROWS = 128
SCAN_STEPS = 64
SCAN_LANES = 1024
SCAN_UNROLL = 4
CAST_SLOTS = 3
CAST_ROWS = 256
CAST_ROWS_WIDE = 128
FFN_CHUNK_TILES = (4, 4, 3)


def _params(n_axes):
    return pltpu.CompilerParams(
        dimension_semantics=("arbitrary",) * n_axes,
        vmem_limit_bytes=VMEM_LIMIT_BYTES)


def _resident(shape):
    return pl.BlockSpec(shape, lambda *_: (0,) * len(shape),
                        pipeline_mode=pl.Buffered(1))


def _tok_spec(tile):
    return pl.BlockSpec((None, tile, D_MODEL), lambda b, s: (b, s, 0))


def _rms(x, g):
    return x * lax.rsqrt(jnp.mean(x * x, axis=-1, keepdims=True) + EPS) * g


def _ln(x):
    mu = jnp.mean(x, axis=-1, keepdims=True)
    xc = x - mu
    return xc * lax.rsqrt(jnp.mean(xc * xc, axis=-1, keepdims=True) + EPS)


def _mm(a, b):
    return jnp.dot(a, b, preferred_element_type=F32)


def _load_bf16(src, dst_ref, stage_ref, sem_ref):
    slots, chunk = stage_ref.shape[0], stage_ref.shape[1]
    n = dst_ref.shape[0] // chunk

    def copy(j):
        return pltpu.make_async_copy(src.at[pl.ds(j * chunk, chunk), :],
                                     stage_ref.at[j % slots], sem_ref.at[j % slots])

    for j in range(min(slots - 1, n)):
        copy(j).start()
    for j in range(n):
        if j + slots - 1 < n:
            copy(j + slots - 1).start()
        copy(j).wait()
        dst_ref[j * chunk:(j + 1) * chunk, :] = stage_ref[j % slots].astype(BF16)


def _stage(chunk_rows, cols):
    return pltpu.VMEM((CAST_SLOTS, chunk_rows, cols), F32)


def _even_kernel(x_ref, g_ref, win_hbm, gw_ref, gb_ref, cw_ref, cb_ref, lng_ref,
                 lnb_ref, wout_hbm, o_ref, hext_ref, conv_ref, win_ref, wout_ref,
                 stage_in, stage_out, sem, *, layer):
    tm = EVEN_TILE
    nblk = tm // GMLP_BLOCK

    @pl.when((pl.program_id(0) == 0) & (pl.program_id(1) == 0))
    def _():
        _load_bf16(win_hbm.at[layer], win_ref, stage_in, sem)
        _load_bf16(wout_hbm.at[layer], wout_ref, stage_out, sem)

    x = x_ref[...]
    hn = _rms(x, g_ref[...]).astype(BF16)
    proj = _mm(hn, win_ref[...])

    u = jax.nn.gelu(proj[:, :A_WIDTH])
    v = _ln(jax.nn.gelu(proj[:, A_WIDTH:2 * A_WIDTH])).astype(BF16)
    ri = lax.broadcasted_iota(jnp.int32, (GMLP_BLOCK, GMLP_BLOCK), 0) >> 6
    ci = lax.broadcasted_iota(jnp.int32, (GMLP_BLOCK, GMLP_BLOCK), 1) >> 6
    keep = ci <= ri
    per_group = []
    for g in range(A_GROUPS):
        gs = slice(g * A_GROUP_DIM, (g + 1) * A_GROUP_DIM)
        rhs = jnp.concatenate(
            [v[nb * GMLP_BLOCK:(nb + 1) * GMLP_BLOCK, gs] for nb in range(nblk)], axis=1)
        wm = jnp.where(keep, gw_ref[g], 0.0).astype(BF16)
        per_group.append(_mm(wm, rhs))
    sg = jnp.concatenate(
        [jnp.concatenate([per_group[g][:, nb * A_GROUP_DIM:(nb + 1) * A_GROUP_DIM]
                          for g in range(A_GROUPS)], axis=1) for nb in range(nblk)], axis=0)
    bias = jnp.concatenate([gb_ref[...]] * nblk, axis=0)
    out_a = u * (sg + bias)

    h = proj[:, 2 * A_WIDTH:2 * A_WIDTH + B_WIDTH] * jax.nn.sigmoid(proj[:, 2 * A_WIDTH + B_WIDTH:])

    @pl.when(pl.program_id(1) == 0)
    def _():
        hext_ref[0:CONV_HALO, :] = jnp.zeros((CONV_HALO, B_WIDTH), F32)

    hext_ref[CONV_HALO:CONV_HALO + tm, :] = h
    for cb in range(B_WIDTH // LANES):
        cs = slice(cb * LANES, (cb + 1) * LANES)
        for rc in range(tm // CONV_ROWS):
            r0 = rc * CONV_ROWS
            win = hext_ref[r0:r0 + CONV_ROWS + CONV_HALO, cs]
            acc = jnp.zeros((CONV_ROWS, LANES), F32)
            for r in range(SUBLANES):
                shifted = win if r == 0 else pltpu.roll(win, r, axis=0)
                for q in range(CONV_HALO // SUBLANES):
                    lag = SUBLANES * q + r
                    if lag < CONV_WIDTH:
                        k = CONV_WIDTH - 1 - lag
                        lo = CONV_HALO - SUBLANES * q
                        acc = acc + cw_ref[k:k + 1, cs] * shifted[lo:lo + CONV_ROWS, :]
            conv_ref[r0:r0 + CONV_ROWS, cs] = acc
    hext_ref[0:CONV_HALO, :] = hext_ref[tm:tm + CONV_HALO, :]
    c = _ln(conv_ref[...] + cb_ref[...]) * lng_ref[...] + lnb_ref[...]
    out_b = jax.nn.silu(c)

    mix = (_mm(out_a.astype(BF16), wout_ref[0:A_WIDTH, :])
           + _mm(out_b.astype(BF16), wout_ref[A_WIDTH:MIX_WIDTH, :]))
    o_ref[...] = x + mix


def _even_layer(x, norm, w_in_all, gmlp_w, gmlp_b, conv_w, conv_b, ln_g, ln_b, w_out_all, layer):
    gb_full = jnp.repeat(gmlp_b.T, A_GROUP_DIM, axis=1)
    row = lambda a: a.reshape(1, -1)
    hbm = pl.BlockSpec(memory_space=pl.ANY)
    return pl.pallas_call(
        functools.partial(_even_kernel, layer=layer),
        grid=(BATCH, SEQ // EVEN_TILE),
        in_specs=[
            _tok_spec(EVEN_TILE),
            _resident((1, D_MODEL)),
            hbm,
            _resident((A_GROUPS, GMLP_BLOCK, GMLP_BLOCK)),
            _resident((GMLP_BLOCK, A_WIDTH)),
            _resident((CONV_WIDTH, B_WIDTH)),
            _resident((1, B_WIDTH)),
            _resident((1, B_WIDTH)),
            _resident((1, B_WIDTH)),
            hbm,
        ],
        out_specs=_tok_spec(EVEN_TILE),
        out_shape=jax.ShapeDtypeStruct((BATCH, SEQ, D_MODEL), F32),
        scratch_shapes=[pltpu.VMEM((CONV_HALO + EVEN_TILE, B_WIDTH), F32),
                        pltpu.VMEM((EVEN_TILE, B_WIDTH), F32),
                        pltpu.VMEM((D_MODEL, IN_WIDTH), BF16),
                        pltpu.VMEM((MIX_WIDTH, D_MODEL), BF16),
                        _stage(CAST_ROWS_WIDE, IN_WIDTH),
                        _stage(CAST_ROWS, D_MODEL),
                        pltpu.SemaphoreType.DMA((CAST_SLOTS,))],
        compiler_params=_params(2),
        name="even_mixer",
    )(x, row(norm), w_in_all, gmlp_w, gb_full, conv_w, row(conv_b), row(ln_g),
      row(ln_b), w_out_all)


def _odd_kernel(x_ref, g_ref, win_hbm, bblk_ref, ar_ref, ai_ref, cblk_ref, d_ref, wout_hbm,
                o_ref, xs_ref, st_ref, win_ref, wout_ref, stage_in, stage_out, sem, *, layer):
    steps = SCAN_STEPS
    rows = steps * BATCH

    @pl.when(pl.program_id(0) == 0)
    def _():
        st_ref[...] = jnp.zeros(st_ref.shape, F32)
        _load_bf16(win_hbm.at[layer], win_ref, stage_in, sem)
        _load_bf16(wout_hbm.at[layer], wout_ref, stage_out, sem)

    x = x_ref[...].reshape(rows, D_MODEL)
    u_bt = _mm(_rms(x, g_ref[...]).astype(BF16), win_ref[...])
    u = jnp.swapaxes(u_bt.reshape(BATCH, steps, C_WIDTH), 0, 1).reshape(rows, C_WIDTH)
    ub = u.astype(BF16)
    for m in range(S5_SPLIT):
        bu = _mm(ub[:, m * S5_CH:(m + 1) * S5_CH], bblk_ref[m])
        xs_ref[:, m * S5_ST:(m + 1) * S5_ST] = bu[:, :S5_ST]
        xs_ref[:, S_LANES + m * S5_ST:S_LANES + (m + 1) * S5_ST] = bu[:, S5_ST:]

    for lb in range(S_LANES // SCAN_LANES):
        re = slice(lb * SCAN_LANES, (lb + 1) * SCAN_LANES)
        im = slice(S_LANES + lb * SCAN_LANES, S_LANES + (lb + 1) * SCAN_LANES)
        ar = jnp.broadcast_to(ar_ref[:, re], (BATCH, SCAN_LANES))
        ai = jnp.broadcast_to(ai_ref[:, re], (BATCH, SCAN_LANES))

        def step(t, carry):
            sr, si = carry
            r = pl.ds(pl.multiple_of(t * BATCH, BATCH), BATCH)
            nr = ar * sr - ai * si + xs_ref[r, re]
            ni = ar * si + ai * sr + xs_ref[r, im]
            xs_ref[r, re] = nr
            xs_ref[r, im] = ni
            return nr, ni

        sr, si = lax.fori_loop(0, steps, step, (st_ref[:, re], st_ref[:, im]),
                               unroll=SCAN_UNROLL)
        st_ref[:, re] = sr
        st_ref[:, im] = si

    ys = []
    for m in range(S5_SPLIT):
        xr = xs_ref[:, m * S5_ST:(m + 1) * S5_ST].astype(BF16)
        xi = xs_ref[:, S_LANES + m * S5_ST:S_LANES + (m + 1) * S5_ST].astype(BF16)
        ys.append(_mm(xr, cblk_ref[m, 0:S5_ST, :]) + _mm(xi, cblk_ref[m, S5_ST:2 * S5_ST, :]))
    y = jnp.concatenate(ys, axis=1) + d_ref[...] * u
    y_bt = jnp.swapaxes(y.reshape(steps, BATCH, C_WIDTH), 0, 1).reshape(rows, C_WIDTH)
    o = _mm(jax.nn.gelu(y_bt).astype(BF16), wout_ref[...])
    out = x + o[:, :D_MODEL] * jax.nn.sigmoid(o[:, D_MODEL:])
    o_ref[...] = out.reshape(BATCH, steps, D_MODEL)


def _s5_discretize(lam_re, lam_im, log_dt, b_re, b_im, c_re, c_im):
    dt = jnp.exp(log_dt)[:, None]
    mag = jnp.exp(lam_re * dt)
    ar = mag * jnp.cos(lam_im * dt)
    ai = mag * jnp.sin(lam_im * dt)
    den = lam_re * lam_re + lam_im * lam_im
    qr = ((ar - 1.0) * lam_re + ai * lam_im) / den
    qi = (ai * lam_re - (ar - 1.0) * lam_im) / den
    bbr = qr[..., None] * b_re - qi[..., None] * b_im
    bbi = qr[..., None] * b_im + qi[..., None] * b_re
    gs = C_GROUPS // S5_SPLIT
    eye = jnp.eye(gs, dtype=F32)
    split = lambda a: a.reshape((S5_SPLIT, gs) + a.shape[1:])
    blk_b = lambda a: jnp.einsum('mgpc,gh->mgchp', split(a), eye).reshape(S5_SPLIT, S5_CH, S5_ST)
    blk_c = lambda a: jnp.einsum('mgcp,gh->mgphc', split(a), eye).reshape(S5_SPLIT, S5_ST, S5_CH)
    bblk = jnp.concatenate([blk_b(bbr), blk_b(bbi)], axis=2)
    cblk = jnp.concatenate([blk_c(c_re), -blk_c(c_im)], axis=1)
    return (bblk.astype(BF16), ar.reshape(1, S_LANES), ai.reshape(1, S_LANES), cblk.astype(BF16))


def _odd_layer(x, norm, w_in_all, ssm, d_skip, w_out_all, layer):
    bblk, ar, ai, cblk = ssm
    spec = pl.BlockSpec((BATCH, SCAN_STEPS, D_MODEL), lambda i: (0, i, 0))
    hbm = pl.BlockSpec(memory_space=pl.ANY)
    return pl.pallas_call(
        functools.partial(_odd_kernel, layer=layer),
        grid=(SEQ // SCAN_STEPS,),
        in_specs=[
            spec,
            _resident((1, D_MODEL)),
            hbm,
            _resident((S5_SPLIT, S5_CH, 2 * S5_ST)),
            _resident((1, S_LANES)),
            _resident((1, S_LANES)),
            _resident((S5_SPLIT, 2 * S5_ST, S5_CH)),
            _resident((1, C_WIDTH)),
            hbm,
        ],
        out_specs=spec,
        out_shape=jax.ShapeDtypeStruct((BATCH, SEQ, D_MODEL), F32),
        scratch_shapes=[pltpu.VMEM((SCAN_STEPS * BATCH, 2 * S_LANES), F32),
                        pltpu.VMEM((BATCH, 2 * S_LANES), F32),
                        pltpu.VMEM((D_MODEL, C_WIDTH), BF16),
                        pltpu.VMEM((C_WIDTH, 2 * D_MODEL), BF16),
                        _stage(CAST_ROWS, C_WIDTH),
                        _stage(CAST_ROWS_WIDE, 2 * D_MODEL),
                        pltpu.SemaphoreType.DMA((CAST_SLOTS,))],
        compiler_params=_params(1),
        name="odd_mixer",
    )(x, norm.reshape(1, -1), w_in_all, bblk, ar, ai, cblk, d_skip.reshape(1, -1), w_out_all)


def _kv_kernel(m_ref, g_ref, wk_hbm, wv_hbm, kt_ref, v_ref, wk_ref, wv_ref, stage, sem, *, layer):
    @pl.when(pl.program_id(0) == 0)
    def _():
        _load_bf16(wk_hbm.at[layer], wk_ref, stage, sem)
        _load_bf16(wv_hbm.at[layer], wv_ref, stage, sem)

    mn = _rms(m_ref[...], g_ref[...]).astype(BF16)
    kt_ref[...] = _mm(mn, wk_ref[...]).T.astype(BF16)
    v_ref[...] = _mm(mn, wv_ref[...]).astype(BF16)


def _square_weight_scratch(n):
    return ([pltpu.VMEM((D_MODEL, D_MODEL), BF16)] * n
            + [_stage(CAST_ROWS, D_MODEL), pltpu.SemaphoreType.DMA((CAST_SLOTS,))])


def _kv_proj(mem, norm, wk_all, wv_all, layer):
    spec = pl.BlockSpec((None, MEM_LEN, D_MODEL), lambda b: (b, 0, 0))
    spec_t = pl.BlockSpec((None, D_MODEL, MEM_LEN), lambda b: (b, 0, 0))
    hbm = pl.BlockSpec(memory_space=pl.ANY)
    return pl.pallas_call(
        functools.partial(_kv_kernel, layer=layer),
        grid=(BATCH,),
        in_specs=[spec, _resident((1, D_MODEL)), hbm, hbm],
        out_specs=[spec_t, spec],
        out_shape=[jax.ShapeDtypeStruct((BATCH, D_MODEL, MEM_LEN), BF16),
                   jax.ShapeDtypeStruct((BATCH, MEM_LEN, D_MODEL), BF16)],
        scratch_shapes=_square_weight_scratch(2),
        compiler_params=_params(1),
        name="ca_kv_proj",
    )(mem, norm.reshape(1, -1), wk_all, wv_all)


def _ca_kernel(x_ref, g_ref, wq_hbm, kt_ref, v_ref, wo_hbm, o_ref, wq_ref, wo_ref, stage, sem,
               *, layer):
    @pl.when((pl.program_id(0) == 0) & (pl.program_id(1) == 0))
    def _():
        _load_bf16(wq_hbm.at[layer], wq_ref, stage, sem)
        _load_bf16(wo_hbm.at[layer], wo_ref, stage, sem)

    x = x_ref[...]
    q = _mm(_rms(x, g_ref[...]).astype(BF16), wq_ref[...]).astype(BF16)
    heads = []
    for h in range(CA_HEADS):
        hs = slice(h * CA_HEAD_DIM, (h + 1) * CA_HEAD_DIM)
        sc = _mm(q[:, hs], kt_ref[hs, :]) * (CA_HEAD_DIM ** -0.5)
        e = jnp.exp(sc - jnp.max(sc, axis=-1, keepdims=True))
        p = e / jnp.sum(e, axis=-1, keepdims=True)
        heads.append(_mm(p.astype(BF16), v_ref[:, hs]))
    o = jnp.concatenate(heads, axis=1).astype(BF16)
    o_ref[...] = x + _mm(o, wo_ref[...])


def _cross_attention(x, kt, v, norm, wq_all, wo_all, layer):
    kt_spec = pl.BlockSpec((None, D_MODEL, MEM_LEN), lambda b, s: (b, 0, 0))
    v_spec = pl.BlockSpec((None, MEM_LEN, D_MODEL), lambda b, s: (b, 0, 0))
    hbm = pl.BlockSpec(memory_space=pl.ANY)
    return pl.pallas_call(
        functools.partial(_ca_kernel, layer=layer),
        grid=(BATCH, SEQ // CA_TILE),
        in_specs=[_tok_spec(CA_TILE), _resident((1, D_MODEL)), hbm, kt_spec, v_spec, hbm],
        out_specs=_tok_spec(CA_TILE),
        out_shape=jax.ShapeDtypeStruct((BATCH, SEQ, D_MODEL), F32),
        scratch_shapes=_square_weight_scratch(2),
        compiler_params=_params(2),
        name="cross_attention",
    )(x, norm.reshape(1, -1), wq_all, kt, v, wo_all)


def _ffn_kernel(x_ref, g_ref, wg_hbm, wu_hbm, wd_hbm, fg_ref, o_ref,
                wg_ref, wu_ref, wd_ref, stage_in, stage_out, sem, *, layer, final):
    @pl.when((pl.program_id(0) == 0) & (pl.program_id(1) == 0))
    def _():
        _load_bf16(wg_hbm.at[layer], wg_ref, stage_in, sem)
        _load_bf16(wu_hbm.at[layer], wu_ref, stage_in, sem)
        _load_bf16(wd_hbm.at[layer], wd_ref, stage_out, sem)

    x = x_ref[...]
    xn = _rms(x, g_ref[...]).astype(BF16)
    y = x
    lo = 0
    for tiles in FFN_CHUNK_TILES:
        cs = slice(lo, lo + tiles * MXU_TILE)
        lo += tiles * MXU_TILE
        h = jax.nn.silu(_mm(xn, wg_ref[:, cs])) * _mm(xn, wu_ref[:, cs])
        y = y + _mm(h.astype(BF16), wd_ref[cs, :])
    if final:
        y = _rms(y, fg_ref[...])
    o_ref[...] = y


def _ffn(x, norm, wg_all, wu_all, wd_all, final_norm, layer, final):
    hbm = pl.BlockSpec(memory_space=pl.ANY)
    return pl.pallas_call(
        functools.partial(_ffn_kernel, layer=layer, final=final),
        grid=(BATCH, SEQ // FFN_TILE),
        in_specs=[_tok_spec(FFN_TILE), _resident((1, D_MODEL)), hbm, hbm, hbm,
                  _resident((1, D_MODEL))],
        out_specs=_tok_spec(FFN_TILE),
        out_shape=jax.ShapeDtypeStruct((BATCH, SEQ, D_MODEL), F32),
        scratch_shapes=[pltpu.VMEM((D_MODEL, FFN_HIDDEN), BF16),
                        pltpu.VMEM((D_MODEL, FFN_HIDDEN), BF16),
                        pltpu.VMEM((FFN_HIDDEN, D_MODEL), BF16),
                        _stage(CAST_ROWS_WIDE, FFN_HIDDEN),
                        _stage(CAST_ROWS, D_MODEL),
                        pltpu.SemaphoreType.DMA((CAST_SLOTS,))],
        compiler_params=_params(2),
        name="ffn_final" if final else "ffn",
    )(x, norm.reshape(1, -1), wg_all, wu_all, wd_all, final_norm.reshape(1, -1))


def kernel(x, mem, e_norm, e_w_in, e_gmlp_w, e_gmlp_b, e_conv_w, e_conv_b, e_conv_ln_g, e_conv_ln_b, e_w_out, o_norm, o_w_in, o_lam_re, o_lam_im, o_log_dt, o_b_re, o_b_im, o_c_re, o_c_im, o_d, o_w_out, ca_norm, ca_mem_norm, ca_wq, ca_wk, ca_wv, ca_wo, ffn_norm, ffn_w_gate, ffn_w_up, ffn_w_down, final_norm):
    for i in range(DEPTH):
        j = i // 2
        if i % 2 == 0:
            x = _even_layer(x, e_norm[j], e_w_in, e_gmlp_w[j], e_gmlp_b[j], e_conv_w[j],
                            e_conv_b[j], e_conv_ln_g[j], e_conv_ln_b[j], e_w_out, layer=j)
        else:
            ssm = _s5_discretize(o_lam_re[j], o_lam_im[j], o_log_dt[j], o_b_re[j], o_b_im[j],
                                 o_c_re[j], o_c_im[j])
            x = _odd_layer(x, o_norm[j], o_w_in, ssm, o_d[j], o_w_out, layer=j)
        kt, v = _kv_proj(mem, ca_mem_norm[i], ca_wk, ca_wv, layer=i)
        x = _cross_attention(x, kt, v, ca_norm[i], ca_wq, ca_wo, layer=i)
        x = _ffn(x, ffn_norm[i], ffn_w_gate, ffn_w_up, ffn_w_down, final_norm,
                 layer=i, final=(i == DEPTH - 1))
    return x
```

```python
import functools
import math

import jax
import jax.numpy as jnp
from jax import lax
from jax.experimental import pallas as pl
from jax.experimental.pallas import tpu as pltpu

F32 = jnp.float32
BF16 = jnp.bfloat16

D_MODEL = 1024
BATCH = 8
SEQ = 4096
DEPTH = 2
CHUNK = 64
MEM_LEN = 256
EPS = 1e-6
A_WIDTH = 512
A_GROUPS = 4
A_GROUP_DIM = A_WIDTH // A_GROUPS
GMLP_BLOCK = 128
B_WIDTH = 512
CONV_WIDTH = 31
MIX_WIDTH = A_WIDTH + B_WIDTH
IN_WIDTH = 2 * A_WIDTH + 2 * B_WIDTH
C_WIDTH = 512
C_GROUP_CH = 16
C_GROUPS = C_WIDTH // C_GROUP_CH
C_STATE = 64
S_LANES = C_GROUPS * C_STATE
S5_SPLIT = 2
S5_CH = C_WIDTH // S5_SPLIT
S5_ST = S_LANES // S5_SPLIT
CA_HEADS = 4
CA_HEAD_DIM = D_MODEL // CA_HEADS
FFN_HIDDEN = -(-8 * D_MODEL // (3 * 256)) * 256

VMEM_LIMIT_BYTES = 56 * 1024 * 1024
SUBLANES = 8
LANES = 128
MXU_TILE = 256

EVEN_TILE = 512
CA_TILE = 1024
FFN_SPLIT = 2
CA_SPLIT = 2
FFN_TILE = 1024
CONV_HALO = 32
CONV_ROWS = 128
SCAN_STEPS = 64
SCAN_LANES = 1024
SCAN_UNROLL = 4
CAST_SLOTS = 3
CAST_ROWS = 256
CAST_ROWS_WIDE = 128
FFN_CHUNK_TILES = (4, 4, 3)


def _params(n_axes):
    return pltpu.CompilerParams(
        dimension_semantics=("arbitrary",) * n_axes,
        vmem_limit_bytes=VMEM_LIMIT_BYTES)


def _resident(shape):
    return pl.BlockSpec(shape, lambda *_: (0,) * len(shape),
                        pipeline_mode=pl.Buffered(1))


def _tok_spec(tile):
    return pl.BlockSpec((None, tile, D_MODEL), lambda b, s: (b, s, 0))


def _rms(x, g):
    return x * lax.rsqrt(jnp.mean(x * x, axis=-1, keepdims=True) + EPS) * g


def _ln(x):
    mu = jnp.mean(x, axis=-1, keepdims=True)
    xc = x - mu
    return xc * lax.rsqrt(jnp.mean(xc * xc, axis=-1, keepdims=True) + EPS)


def _mm(a, b):
    return jnp.dot(a, b, preferred_element_type=F32)


def _load_bf16(src, dst_ref, stage_ref, sem_ref):
    slots, chunk = stage_ref.shape[0], stage_ref.shape[1]
    n = dst_ref.shape[0] // chunk

    def copy(j):
        return pltpu.make_async_copy(src.at[pl.ds(j * chunk, chunk), :],
                                     stage_ref.at[j % slots], sem_ref.at[j % slots])

    for j in range(min(slots - 1, n)):
        copy(j).start()
    for j in range(n):
        if j + slots - 1 < n:
            copy(j + slots - 1).start()
        copy(j).wait()
        dst_ref[j * chunk:(j + 1) * chunk, :] = stage_ref[j % slots].astype(BF16)


def _stage(chunk_rows, cols):
    return pltpu.VMEM((CAST_SLOTS, chunk_rows, cols), F32)


def _even_kernel(x_ref, g_ref, win_hbm, gw_ref, gb_ref, cw_ref, cb_ref, lng_ref,
                 lnb_ref, wout_hbm, o_ref, hext_ref, conv_ref, win_ref, wout_ref,
                 stage_in, stage_out, sem, *, layer):
    tm = EVEN_TILE
    nblk = tm // GMLP_BLOCK

    @pl.when((pl.program_id(0) == 0) & (pl.program_id(1) == 0))
    def _():
        _load_bf16(win_hbm.at[layer], win_ref, stage_in, sem)
        _load_bf16(wout_hbm.at[layer], wout_ref, stage_out, sem)

    x = x_ref[...]
    hn = _rms(x, g_ref[...]).astype(BF16)
    proj = _mm(hn, win_ref[...])

    u = jax.nn.gelu(proj[:, :A_WIDTH])
    v = _ln(jax.nn.gelu(proj[:, A_WIDTH:2 * A_WIDTH])).astype(BF16)
    ri = lax.broadcasted_iota(jnp.int32, (GMLP_BLOCK, GMLP_BLOCK), 0) >> 6
    ci = lax.broadcasted_iota(jnp.int32, (GMLP_BLOCK, GMLP_BLOCK), 1) >> 6
    keep = ci <= ri
    per_group = []
    for g in range(A_GROUPS):
        gs = slice(g * A_GROUP_DIM, (g + 1) * A_GROUP_DIM)
        rhs = jnp.concatenate(
            [v[nb * GMLP_BLOCK:(nb + 1) * GMLP_BLOCK, gs] for nb in range(nblk)], axis=1)
        wm = jnp.where(keep, gw_ref[g], 0.0).astype(BF16)
        per_group.append(_mm(wm, rhs))
    sg = jnp.concatenate(
        [jnp.concatenate([per_group[g][:, nb * A_GROUP_DIM:(nb + 1) * A_GROUP_DIM]
                          for g in range(A_GROUPS)], axis=1) for nb in range(nblk)], axis=0)
    bias = jnp.concatenate([gb_ref[...]] * nblk, axis=0)
    out_a = u * (sg + bias)

    h = proj[:, 2 * A_WIDTH:2 * A_WIDTH + B_WIDTH] * jax.nn.sigmoid(proj[:, 2 * A_WIDTH + B_WIDTH:])

    @pl.when(pl.program_id(1) == 0)
    def _():
        hext_ref[0:CONV_HALO, :] = jnp.zeros((CONV_HALO, B_WIDTH), F32)

    hext_ref[CONV_HALO:CONV_HALO + tm, :] = h
    for cb in range(B_WIDTH // LANES):
        cs = slice(cb * LANES, (cb + 1) * LANES)
        for rc in range(tm // CONV_ROWS):
            r0 = rc * CONV_ROWS
            win = hext_ref[r0:r0 + CONV_ROWS + CONV_HALO, cs]
            acc = jnp.zeros((CONV_ROWS, LANES), F32)
            for r in range(SUBLANES):
                shifted = win if r == 0 else pltpu.roll(win, r, axis=0)
                for q in range(CONV_HALO // SUBLANES):
                    lag = SUBLANES * q + r
                    if lag < CONV_WIDTH:
                        k = CONV_WIDTH - 1 - lag
                        lo = CONV_HALO - SUBLANES * q
                        acc = acc + cw_ref[k:k + 1, cs] * shifted[lo:lo + CONV_ROWS, :]
            conv_ref[r0:r0 + CONV_ROWS, cs] = acc
    hext_ref[0:CONV_HALO, :] = hext_ref[tm:tm + CONV_HALO, :]
    c = _ln(conv_ref[...] + cb_ref[...]) * lng_ref[...] + lnb_ref[...]
    out_b = jax.nn.silu(c)

    mix = (_mm(out_a.astype(BF16), wout_ref[0:A_WIDTH, :])
           + _mm(out_b.astype(BF16), wout_ref[A_WIDTH:MIX_WIDTH, :]))
    o_ref[...] = x + mix


def _even_layer(x, norm, w_in_all, gmlp_w, gmlp_b, conv_w, conv_b, ln_g, ln_b, w_out_all, layer):
    gb_full = jnp.repeat(gmlp_b.T, A_GROUP_DIM, axis=1)
    row = lambda a: a.reshape(1, -1)
    hbm = pl.BlockSpec(memory_space=pl.ANY)
    return pl.pallas_call(
        functools.partial(_even_kernel, layer=layer),
        grid=(BATCH, SEQ // EVEN_TILE),
        in_specs=[
            _tok_spec(EVEN_TILE),
            _resident((1, D_MODEL)),
            hbm,
            _resident((A_GROUPS, GMLP_BLOCK, GMLP_BLOCK)),
            _resident((GMLP_BLOCK, A_WIDTH)),
            _resident((CONV_WIDTH, B_WIDTH)),
            _resident((1, B_WIDTH)),
            _resident((1, B_WIDTH)),
            _resident((1, B_WIDTH)),
            hbm,
        ],
        out_specs=_tok_spec(EVEN_TILE),
        out_shape=jax.ShapeDtypeStruct((BATCH, SEQ, D_MODEL), F32),
        scratch_shapes=[pltpu.VMEM((CONV_HALO + EVEN_TILE, B_WIDTH), F32),
                        pltpu.VMEM((EVEN_TILE, B_WIDTH), F32),
                        pltpu.VMEM((D_MODEL, IN_WIDTH), BF16),
                        pltpu.VMEM((MIX_WIDTH, D_MODEL), BF16),
                        _stage(CAST_ROWS_WIDE, IN_WIDTH),
                        _stage(CAST_ROWS, D_MODEL),
                        pltpu.SemaphoreType.DMA((CAST_SLOTS,))],
        compiler_params=_params(2),
        name="even_mixer",
    )(x, row(norm), w_in_all, gmlp_w, gb_full, conv_w, row(conv_b), row(ln_g),
      row(ln_b), w_out_all)


def _odd_kernel(x_ref, g_ref, win_hbm, bblk_ref, ar_ref, ai_ref, cblk_ref, d_ref, wout_hbm,
                o_ref, xs_ref, st_ref, win_ref, wout_ref, stage_in, stage_out, sem, *, layer):
    steps = SCAN_STEPS
    rows = steps * BATCH

    @pl.when(pl.program_id(0) == 0)
    def _():
        st_ref[...] = jnp.zeros(st_ref.shape, F32)
        _load_bf16(win_hbm.at[layer], win_ref, stage_in, sem)
        _load_bf16(wout_hbm.at[layer], wout_ref, stage_out, sem)

    x = x_ref[...].reshape(rows, D_MODEL)
    u_bt = _mm(_rms(x, g_ref[...]).astype(BF16), win_ref[...])
    u = jnp.swapaxes(u_bt.reshape(BATCH, steps, C_WIDTH), 0, 1).reshape(rows, C_WIDTH)
    ub = u.astype(BF16)
    for m in range(S5_SPLIT):
        bu = _mm(ub[:, m * S5_CH:(m + 1) * S5_CH], bblk_ref[m])
        xs_ref[:, m * S5_ST:(m + 1) * S5_ST] = bu[:, :S5_ST]
        xs_ref[:, S_LANES + m * S5_ST:S_LANES + (m + 1) * S5_ST] = bu[:, S5_ST:]

    for lb in range(S_LANES // SCAN_LANES):
        re = slice(lb * SCAN_LANES, (lb + 1) * SCAN_LANES)
        im = slice(S_LANES + lb * SCAN_LANES, S_LANES + (lb + 1) * SCAN_LANES)
        ar = jnp.broadcast_to(ar_ref[:, re], (BATCH, SCAN_LANES))
        ai = jnp.broadcast_to(ai_ref[:, re], (BATCH, SCAN_LANES))

        def step(t, carry):
            sr, si = carry
            r = pl.ds(pl.multiple_of(t * BATCH, BATCH), BATCH)
            nr = ar * sr - ai * si + xs_ref[r, re]
            ni = ar * si + ai * sr + xs_ref[r, im]
            xs_ref[r, re] = nr
            xs_ref[r, im] = ni
            return nr, ni

        sr, si = lax.fori_loop(0, steps, step, (st_ref[:, re], st_ref[:, im]),
                               unroll=SCAN_UNROLL)
        st_ref[:, re] = sr
        st_ref[:, im] = si

    ys = []
    for m in range(S5_SPLIT):
        xr = xs_ref[:, m * S5_ST:(m + 1) * S5_ST].astype(BF16)
        xi = xs_ref[:, S_LANES + m * S5_ST:S_LANES + (m + 1) * S5_ST].astype(BF16)
        ys.append(_mm(xr, cblk_ref[m, 0:S5_ST, :]) + _mm(xi, cblk_ref[m, S5_ST:2 * S5_ST, :]))
    y = jnp.concatenate(ys, axis=1) + d_ref[...] * u
    y_bt = jnp.swapaxes(y.reshape(steps, BATCH, C_WIDTH), 0, 1).reshape(rows, C_WIDTH)
    o = _mm(jax.nn.gelu(y_bt).astype(BF16), wout_ref[...])
    out = x + o[:, :D_MODEL] * jax.nn.sigmoid(o[:, D_MODEL:])
    o_ref[...] = out.reshape(BATCH, steps, D_MODEL)


def _s5_discretize(lam_re, lam_im, log_dt, b_re, b_im, c_re, c_im):
    dt = jnp.exp(log_dt)[:, None]
    mag = jnp.exp(lam_re * dt)
    ar = mag * jnp.cos(lam_im * dt)
    ai = mag * jnp.sin(lam_im * dt)
    den = lam_re * lam_re + lam_im * lam_im
    qr = ((ar - 1.0) * lam_re + ai * lam_im) / den
    qi = (ai * lam_re - (ar - 1.0) * lam_im) / den
    bbr = qr[..., None] * b_re - qi[..., None] * b_im
    bbi = qr[..., None] * b_im + qi[..., None] * b_re
    gs = C_GROUPS // S5_SPLIT
    eye = jnp.eye(gs, dtype=F32)
    split = lambda a: a.reshape((S5_SPLIT, gs) + a.shape[1:])
    blk_b = lambda a: jnp.einsum('mgpc,gh->mgchp', split(a), eye).reshape(S5_SPLIT, S5_CH, S5_ST)
    blk_c = lambda a: jnp.einsum('mgcp,gh->mgphc', split(a), eye).reshape(S5_SPLIT, S5_ST, S5_CH)
    bblk = jnp.concatenate([blk_b(bbr), blk_b(bbi)], axis=2)
    cblk = jnp.concatenate([blk_c(c_re), -blk_c(c_im)], axis=1)
    return (bblk.astype(BF16), ar.reshape(1, S_LANES), ai.reshape(1, S_LANES), cblk.astype(BF16))


def _odd_layer(x, norm, w_in_all, ssm, d_skip, w_out_all, layer):
    bblk, ar, ai, cblk = ssm
    spec = pl.BlockSpec((BATCH, SCAN_STEPS, D_MODEL), lambda i: (0, i, 0))
    hbm = pl.BlockSpec(memory_space=pl.ANY)
    return pl.pallas_call(
        functools.partial(_odd_kernel, layer=layer),
        grid=(SEQ // SCAN_STEPS,),
        in_specs=[
            spec,
            _resident((1, D_MODEL)),
            hbm,
            _resident((S5_SPLIT, S5_CH, 2 * S5_ST)),
            _resident((1, S_LANES)),
            _resident((1, S_LANES)),
            _resident((S5_SPLIT, 2 * S5_ST, S5_CH)),
            _resident((1, C_WIDTH)),
            hbm,
        ],
        out_specs=spec,
        out_shape=jax.ShapeDtypeStruct((BATCH, SEQ, D_MODEL), F32),
        scratch_shapes=[pltpu.VMEM((SCAN_STEPS * BATCH, 2 * S_LANES), F32),
                        pltpu.VMEM((BATCH, 2 * S_LANES), F32),
                        pltpu.VMEM((D_MODEL, C_WIDTH), BF16),
                        pltpu.VMEM((C_WIDTH, 2 * D_MODEL), BF16),
                        _stage(CAST_ROWS, C_WIDTH),
                        _stage(CAST_ROWS_WIDE, 2 * D_MODEL),
                        pltpu.SemaphoreType.DMA((CAST_SLOTS,))],
        compiler_params=_params(1),
        name="odd_mixer",
    )(x, norm.reshape(1, -1), w_in_all, bblk, ar, ai, cblk, d_skip.reshape(1, -1), w_out_all)


def _kv_kernel(m_ref, g_ref, wk_hbm, wv_hbm, kt_ref, v_ref, wk_ref, wv_ref, stage, sem, *, layer):
    @pl.when(pl.program_id(0) == 0)
    def _():
        _load_bf16(wk_hbm.at[layer], wk_ref, stage, sem)
        _load_bf16(wv_hbm.at[layer], wv_ref, stage, sem)

    mn = _rms(m_ref[...], g_ref[...]).astype(BF16)
    kt_ref[...] = _mm(mn, wk_ref[...]).T.astype(BF16)
    v_ref[...] = _mm(mn, wv_ref[...]).astype(BF16)


def _square_weight_scratch(n):
    return ([pltpu.VMEM((D_MODEL, D_MODEL), BF16)] * n
            + [_stage(CAST_ROWS, D_MODEL), pltpu.SemaphoreType.DMA((CAST_SLOTS,))])


def _kv_proj(mem, norm, wk_all, wv_all, layer):
    spec = pl.BlockSpec((None, MEM_LEN, D_MODEL), lambda b: (b, 0, 0))
    spec_t = pl.BlockSpec((None, D_MODEL, MEM_LEN), lambda b: (b, 0, 0))
    hbm = pl.BlockSpec(memory_space=pl.ANY)
    return pl.pallas_call(
        functools.partial(_kv_kernel, layer=layer),
        grid=(BATCH,),
        in_specs=[spec, _resident((1, D_MODEL)), hbm, hbm],
        out_specs=[spec_t, spec],
        out_shape=[jax.ShapeDtypeStruct((BATCH, D_MODEL, MEM_LEN), BF16),
                   jax.ShapeDtypeStruct((BATCH, MEM_LEN, D_MODEL), BF16)],
        scratch_shapes=_square_weight_scratch(2),
        compiler_params=_params(1),
        name="ca_kv_proj",
    )(mem, norm.reshape(1, -1), wk_all, wv_all)


def _ca_kernel(x_ref, g_ref, wq_hbm, kt_ref, v_ref, wo_hbm, o_ref, wq_ref, wo_ref, stage, sem,
               *, layer):
    @pl.when((pl.program_id(0) == 0) & (pl.program_id(1) == 0))
    def _():
        _load_bf16(wq_hbm.at[layer], wq_ref, stage, sem)
        _load_bf16(wo_hbm.at[layer], wo_ref, stage, sem)

    rows = CA_TILE // CA_SPLIT
    for blk in range(CA_SPLIT):
        rs = slice(blk * rows, (blk + 1) * rows)
        x = x_ref[rs, :]
        q = _mm(_rms(x, g_ref[...]).astype(BF16), wq_ref[...]).astype(BF16)
        heads = []
        for h in range(CA_HEADS):
            hs = slice(h * CA_HEAD_DIM, (h + 1) * CA_HEAD_DIM)
            sc = _mm(q[:, hs], kt_ref[hs, :]) * (CA_HEAD_DIM ** -0.5)
            e = jnp.exp(sc - jnp.max(sc, axis=-1, keepdims=True))
            heads.append(_mm(e.astype(BF16), v_ref[:, hs]) / jnp.sum(e, axis=-1, keepdims=True))
        o = jnp.concatenate(heads, axis=1).astype(BF16)
        o_ref[rs, :] = x + _mm(o, wo_ref[...])


def _cross_attention(x, kt, v, norm, wq_all, wo_all, layer):
    kt_spec = pl.BlockSpec((None, D_MODEL, MEM_LEN), lambda b, s: (b, 0, 0))
    v_spec = pl.BlockSpec((None, MEM_LEN, D_MODEL), lambda b, s: (b, 0, 0))
    hbm = pl.BlockSpec(memory_space=pl.ANY)
    return pl.pallas_call(
        functools.partial(_ca_kernel, layer=layer),
        grid=(BATCH, SEQ // CA_TILE),
        in_specs=[_tok_spec(CA_TILE), _resident((1, D_MODEL)), hbm, kt_spec, v_spec, hbm],
        out_specs=_tok_spec(CA_TILE),
        out_shape=jax.ShapeDtypeStruct((BATCH, SEQ, D_MODEL), F32),
        scratch_shapes=_square_weight_scratch(2),
        compiler_params=_params(2),
        name="cross_attention",
    )(x, norm.reshape(1, -1), wq_all, kt, v, wo_all)


def _ffn_kernel(x_ref, g_ref, wg_hbm, wu_hbm, wd_hbm, fg_ref, o_ref,
                wg_ref, wu_ref, wd_ref, stage_in, stage_out, sem, *, layer, final):
    @pl.when((pl.program_id(0) == 0) & (pl.program_id(1) == 0))
    def _():
        _load_bf16(wg_hbm.at[layer], wg_ref, stage_in, sem)
        _load_bf16(wu_hbm.at[layer], wu_ref, stage_in, sem)
        _load_bf16(wd_hbm.at[layer], wd_ref, stage_out, sem)

    rows = FFN_TILE // FFN_SPLIT
    for blk in range(FFN_SPLIT):
        rs = slice(blk * rows, (blk + 1) * rows)
        x = x_ref[rs, :]
        xn = _rms(x, g_ref[...]).astype(BF16)
        y = x
        lo = 0
        for tiles in FFN_CHUNK_TILES:
            cs = slice(lo, lo + tiles * MXU_TILE)
            lo += tiles * MXU_TILE
            h = jax.nn.silu(_mm(xn, wg_ref[:, cs])) * _mm(xn, wu_ref[:, cs])
            y = y + _mm(h.astype(BF16), wd_ref[cs, :])
        if final:
            y = _rms(y, fg_ref[...])
        o_ref[rs, :] = y


def _ffn(x, norm, wg_all, wu_all, wd_all, final_norm, layer, final):
    hbm = pl.BlockSpec(memory_space=pl.ANY)
    return pl.pallas_call(
        functools.partial(_ffn_kernel, layer=layer, final=final),
        grid=(BATCH, SEQ // FFN_TILE),
        in_specs=[_tok_spec(FFN_TILE), _resident((1, D_MODEL)), hbm, hbm, hbm,
                  _resident((1, D_MODEL))],
        out_specs=_tok_spec(FFN_TILE),
        out_shape=jax.ShapeDtypeStruct((BATCH, SEQ, D_MODEL), F32),
        scratch_shapes=[pltpu.VMEM((D_MODEL, FFN_HIDDEN), BF16),
                        pltpu.VMEM((D_MODEL, FFN_HIDDEN), BF16),
                        pltpu.VMEM((FFN_HIDDEN, D_MODEL), BF16),
                        _stage(CAST_ROWS_WIDE, FFN_HIDDEN),
                        _stage(CAST_ROWS, D_MODEL),
                        pltpu.SemaphoreType.DMA((CAST_SLOTS,))],
        compiler_params=_params(2),
        name="ffn_final" if final else "ffn",
    )(x, norm.reshape(1, -1), wg_all, wu_all, wd_all, final_norm.reshape(1, -1))


def kernel(x, mem, e_norm, e_w_in, e_gmlp_w, e_gmlp_b, e_conv_w, e_conv_b, e_conv_ln_g, e_conv_ln_b, e_w_out, o_norm, o_w_in, o_lam_re, o_lam_im, o_log_dt, o_b_re, o_b_im, o_c_re, o_c_im, o_d, o_w_out, ca_norm, ca_mem_norm, ca_wq, ca_wk, ca_wv, ca_wo, ffn_norm, ffn_w_gate, ffn_w_up, ffn_w_down, final_norm):
    for i in range(DEPTH):
        j = i // 2
        if i % 2 == 0:
            x = _even_layer(x, e_norm[j], e_w_in, e_gmlp_w[j], e_gmlp_b[j], e_conv_w[j],
                            e_conv_b[j], e_conv_ln_g[j], e_conv_ln_b[j], e_w_out, layer=j)
        else:
            ssm = _s5_discretize(o_lam_re[j], o_lam_im[j], o_log_dt[j], o_b_re[j], o_b_im[j],
                                 o_c_re[j], o_c_im[j])
            x = _odd_layer(x, o_norm[j], o_w_in, ssm, o_d[j], o_w_out, layer=j)
        kt, v = _kv_proj(mem, ca_mem_norm[i], ca_wk, ca_wv, layer=i)
        x = _cross_attention(x, kt, v, ca_norm[i], ca_wq, ca_wo, layer=i)
        x = _ffn(x, ffn_norm[i], ffn_w_gate, ffn_w_up, ffn_w_down, final_norm,
                 layer=i, final=(i == DEPTH - 1))
    return x
```

```python
import functools
import math

import jax
import jax.numpy as jnp
from jax import lax
from jax.experimental import pallas as pl
from jax.experimental.pallas import tpu as pltpu

F32 = jnp.float32
BF16 = jnp.bfloat16

D_MODEL = 1024
BATCH = 8
SEQ = 4096
DEPTH = 2
CHUNK = 64
MEM_LEN = 256
EPS = 1e-6
A_WIDTH = 512
A_GROUPS = 4
A_GROUP_DIM = A_WIDTH // A_GROUPS
GMLP_BLOCK = 128
B_WIDTH = 512
CONV_WIDTH = 31
MIX_WIDTH = A_WIDTH + B_WIDTH
IN_WIDTH = 2 * A_WIDTH + 2 * B_WIDTH
C_WIDTH = 512
C_GROUP_CH = 16
C_GROUPS = C_WIDTH // C_GROUP_CH
C_STATE = 64
S_LANES = C_GROUPS * C_STATE
S5_SPLIT = 2
S5_CH = C_WIDTH // S5_SPLIT
S5_ST = S_LANES // S5_SPLIT
CA_HEADS = 4
CA_HEAD_DIM = D_MODEL // CA_HEADS
FFN_HIDDEN = -(-8 * D_MODEL // (3 * 256)) * 256

VMEM_LIMIT_BYTES = 56 * 1024 * 1024
SUBLANES = 8
LANES = 128
MXU_TILE = 256

EVEN_TILE = 512
CA_TILE = 2048
FFN_SPLIT = 2
CA_SPLIT = 4
FFN_TILE = 1024
CONV_HALO = 32
CONV_ROWS = 128
SCAN_STEPS = 64
SCAN_LANES = 1024
SCAN_UNROLL = 4
CAST_SLOTS = 3
CAST_ROWS = 256
CAST_ROWS_WIDE = 128
FFN_CHUNK_TILES = (11,)


def _params(n_axes):
    return pltpu.CompilerParams(
        dimension_semantics=("arbitrary",) * n_axes,
        vmem_limit_bytes=VMEM_LIMIT_BYTES)


def _resident(shape):
    return pl.BlockSpec(shape, lambda *_: (0,) * len(shape),
                        pipeline_mode=pl.Buffered(1))


def _tok_spec(tile):
    return pl.BlockSpec((None, tile, D_MODEL), lambda b, s: (b, s, 0))


def _rms(x, g):
    return x * lax.rsqrt(jnp.mean(x * x, axis=-1, keepdims=True) + EPS) * g


def _ln(x):
    mu = jnp.mean(x, axis=-1, keepdims=True)
    xc = x - mu
    return xc * lax.rsqrt(jnp.mean(xc * xc, axis=-1, keepdims=True) + EPS)


def _mm(a, b):
    return jnp.dot(a, b, preferred_element_type=F32)


def _load_bf16(src, dst_ref, stage_ref, sem_ref):
    slots, chunk = stage_ref.shape[0], stage_ref.shape[1]
    n = dst_ref.shape[0] // chunk

    def copy(j):
        return pltpu.make_async_copy(src.at[pl.ds(j * chunk, chunk), :],
                                     stage_ref.at[j % slots], sem_ref.at[j % slots])

    for j in range(min(slots - 1, n)):
        copy(j).start()
    for j in range(n):
        if j + slots - 1 < n:
            copy(j + slots - 1).start()
        copy(j).wait()
        dst_ref[j * chunk:(j + 1) * chunk, :] = stage_ref[j % slots].astype(BF16)


def _stage(chunk_rows, cols):
    return pltpu.VMEM((CAST_SLOTS, chunk_rows, cols), F32)


def _even_kernel(x_ref, g_ref, win_hbm, gw_ref, gb_ref, cw_ref, cb_ref, lng_ref,
                 lnb_ref, wout_hbm, o_ref, hext_ref, conv_ref, win_ref, wout_ref,
                 stage_in, stage_out, sem, *, layer):
    tm = EVEN_TILE
    nblk = tm // GMLP_BLOCK

    @pl.when((pl.program_id(0) == 0) & (pl.program_id(1) == 0))
    def _():
        _load_bf16(win_hbm.at[layer], win_ref, stage_in, sem)
        _load_bf16(wout_hbm.at[layer], wout_ref, stage_out, sem)

    x = x_ref[...]
    hn = _rms(x, g_ref[...]).astype(BF16)
    proj = _mm(hn, win_ref[...])

    u = jax.nn.gelu(proj[:, :A_WIDTH])
    v = _ln(jax.nn.gelu(proj[:, A_WIDTH:2 * A_WIDTH])).astype(BF16)
    ri = lax.broadcasted_iota(jnp.int32, (GMLP_BLOCK, GMLP_BLOCK), 0) >> 6
    ci = lax.broadcasted_iota(jnp.int32, (GMLP_BLOCK, GMLP_BLOCK), 1) >> 6
    keep = ci <= ri
    per_group = []
    for g in range(A_GROUPS):
        gs = slice(g * A_GROUP_DIM, (g + 1) * A_GROUP_DIM)
        rhs = jnp.concatenate(
            [v[nb * GMLP_BLOCK:(nb + 1) * GMLP_BLOCK, gs] for nb in range(nblk)], axis=1)
        wm = jnp.where(keep, gw_ref[g], 0.0).astype(BF16)
        per_group.append(_mm(wm, rhs))
    sg = jnp.concatenate(
        [jnp.concatenate([per_group[g][:, nb * A_GROUP_DIM:(nb + 1) * A_GROUP_DIM]
                          for g in range(A_GROUPS)], axis=1) for nb in range(nblk)], axis=0)
    bias = jnp.concatenate([gb_ref[...]] * nblk, axis=0)
    out_a = u * (sg + bias)

    h = proj[:, 2 * A_WIDTH:2 * A_WIDTH + B_WIDTH] * jax.nn.sigmoid(proj[:, 2 * A_WIDTH + B_WIDTH:])

    @pl.when(pl.program_id(1) == 0)
    def _():
        hext_ref[0:CONV_HALO, :] = jnp.zeros((CONV_HALO, B_WIDTH), F32)

    hext_ref[CONV_HALO:CONV_HALO + tm, :] = h
    for cb in range(B_WIDTH // LANES):
        cs = slice(cb * LANES, (cb + 1) * LANES)
        for rc in range(tm // CONV_ROWS):
            r0 = rc * CONV_ROWS
            win = hext_ref[r0:r0 + CONV_ROWS + CONV_HALO, cs]
            acc = jnp.zeros((CONV_ROWS, LANES), F32)
            for r in range(SUBLANES):
                shifted = win if r == 0 else pltpu.roll(win, r, axis=0)
                for q in range(CONV_HALO // SUBLANES):
                    lag = SUBLANES * q + r
                    if lag < CONV_WIDTH:
                        k = CONV_WIDTH - 1 - lag
                        lo = CONV_HALO - SUBLANES * q
                        acc = acc + cw_ref[k:k + 1, cs] * shifted[lo:lo + CONV_ROWS, :]
            conv_ref[r0:r0 + CONV_ROWS, cs] = acc
    hext_ref[0:CONV_HALO, :] = hext_ref[tm:tm + CONV_HALO, :]
    c = _ln(conv_ref[...] + cb_ref[...]) * lng_ref[...] + lnb_ref[...]
    out_b = jax.nn.silu(c)

    mix = (_mm(out_a.astype(BF16), wout_ref[0:A_WIDTH, :])
           + _mm(out_b.astype(BF16), wout_ref[A_WIDTH:MIX_WIDTH, :]))
    o_ref[...] = x + mix


def _even_layer(x, norm, w_in_all, gmlp_w, gmlp_b, conv_w, conv_b, ln_g, ln_b, w_out_all, layer):
    gb_full = jnp.repeat(gmlp_b.T, A_GROUP_DIM, axis=1)
    row = lambda a: a.reshape(1, -1)
    hbm = pl.BlockSpec(memory_space=pl.ANY)
    return pl.pallas_call(
        functools.partial(_even_kernel, layer=layer),
        grid=(BATCH, SEQ // EVEN_TILE),
        in_specs=[
            _tok_spec(EVEN_TILE),
            _resident((1, D_MODEL)),
            hbm,
            _resident((A_GROUPS, GMLP_BLOCK, GMLP_BLOCK)),
            _resident((GMLP_BLOCK, A_WIDTH)),
            _resident((CONV_WIDTH, B_WIDTH)),
            _resident((1, B_WIDTH)),
            _resident((1, B_WIDTH)),
            _resident((1, B_WIDTH)),
            hbm,
        ],
        out_specs=_tok_spec(EVEN_TILE),
        out_shape=jax.ShapeDtypeStruct((BATCH, SEQ, D_MODEL), F32),
        scratch_shapes=[pltpu.VMEM((CONV_HALO + EVEN_TILE, B_WIDTH), F32),
                        pltpu.VMEM((EVEN_TILE, B_WIDTH), F32),
                        pltpu.VMEM((D_MODEL, IN_WIDTH), BF16),
                        pltpu.VMEM((MIX_WIDTH, D_MODEL), BF16),
                        _stage(CAST_ROWS_WIDE, IN_WIDTH),
                        _stage(CAST_ROWS, D_MODEL),
                        pltpu.SemaphoreType.DMA((CAST_SLOTS,))],
        compiler_params=_params(2),
        name="even_mixer",
    )(x, row(norm), w_in_all, gmlp_w, gb_full, conv_w, row(conv_b), row(ln_g),
      row(ln_b), w_out_all)


def _odd_kernel(x_ref, g_ref, win_hbm, bblk_ref, ar_ref, ai_ref, cblk_ref, d_ref, wout_hbm,
                o_ref, xs_ref, st_ref, win_ref, wout_ref, stage_in, stage_out, sem, *, layer):
    steps = SCAN_STEPS
    rows = steps * BATCH

    @pl.when(pl.program_id(0) == 0)
    def _():
        st_ref[...] = jnp.zeros(st_ref.shape, F32)
        _load_bf16(win_hbm.at[layer], win_ref, stage_in, sem)
        _load_bf16(wout_hbm.at[layer], wout_ref, stage_out, sem)

    x = x_ref[...].reshape(rows, D_MODEL)
    u_bt = _mm(_rms(x, g_ref[...]).astype(BF16), win_ref[...])
    u = jnp.swapaxes(u_bt.reshape(BATCH, steps, C_WIDTH), 0, 1).reshape(rows, C_WIDTH)
    ub = u.astype(BF16)
    for m in range(S5_SPLIT):
        bu = _mm(ub[:, m * S5_CH:(m + 1) * S5_CH], bblk_ref[m])
        xs_ref[:, m * S5_ST:(m + 1) * S5_ST] = bu[:, :S5_ST]
        xs_ref[:, S_LANES + m * S5_ST:S_LANES + (m + 1) * S5_ST] = bu[:, S5_ST:]

    for lb in range(S_LANES // SCAN_LANES):
        re = slice(lb * SCAN_LANES, (lb + 1) * SCAN_LANES)
        im = slice(S_LANES + lb * SCAN_LANES, S_LANES + (lb + 1) * SCAN_LANES)
        ar = jnp.broadcast_to(ar_ref[:, re], (BATCH, SCAN_LANES))
        ai = jnp.broadcast_to(ai_ref[:, re], (BATCH, SCAN_LANES))

        def step(t, carry):
            sr, si = carry
            r = pl.ds(pl.multiple_of(t * BATCH, BATCH), BATCH)
            nr = ar * sr - ai * si + xs_ref[r, re]
            ni = ar * si + ai * sr + xs_ref[r, im]
            xs_ref[r, re] = nr
            xs_ref[r, im] = ni
            return nr, ni

        sr, si = lax.fori_loop(0, steps, step, (st_ref[:, re], st_ref[:, im]),
                               unroll=SCAN_UNROLL)
        st_ref[:, re] = sr
        st_ref[:, im] = si

    ys = []
    for m in range(S5_SPLIT):
        xr = xs_ref[:, m * S5_ST:(m + 1) * S5_ST].astype(BF16)
        xi = xs_ref[:, S_LANES + m * S5_ST:S_LANES + (m + 1) * S5_ST].astype(BF16)
        ys.append(_mm(xr, cblk_ref[m, 0:S5_ST, :]) + _mm(xi, cblk_ref[m, S5_ST:2 * S5_ST, :]))
    y = jnp.concatenate(ys, axis=1) + d_ref[...] * u
    y_bt = jnp.swapaxes(y.reshape(steps, BATCH, C_WIDTH), 0, 1).reshape(rows, C_WIDTH)
    o = _mm(jax.nn.gelu(y_bt).astype(BF16), wout_ref[...])
    out = x + o[:, :D_MODEL] * jax.nn.sigmoid(o[:, D_MODEL:])
    o_ref[...] = out.reshape(BATCH, steps, D_MODEL)


def _s5_discretize(lam_re, lam_im, log_dt, b_re, b_im, c_re, c_im):
    dt = jnp.exp(log_dt)[:, None]
    mag = jnp.exp(lam_re * dt)
    ar = mag * jnp.cos(lam_im * dt)
    ai = mag * jnp.sin(lam_im * dt)
    den = lam_re * lam_re + lam_im * lam_im
    qr = ((ar - 1.0) * lam_re + ai * lam_im) / den
    qi = (ai * lam_re - (ar - 1.0) * lam_im) / den
    bbr = qr[..., None] * b_re - qi[..., None] * b_im
    bbi = qr[..., None] * b_im + qi[..., None] * b_re
    gs = C_GROUPS // S5_SPLIT
    eye = jnp.eye(gs, dtype=F32)
    split = lambda a: a.reshape((S5_SPLIT, gs) + a.shape[1:])
    blk_b = lambda a: jnp.einsum('mgpc,gh->mgchp', split(a), eye).reshape(S5_SPLIT, S5_CH, S5_ST)
    blk_c = lambda a: jnp.einsum('mgcp,gh->mgphc', split(a), eye).reshape(S5_SPLIT, S5_ST, S5_CH)
    bblk = jnp.concatenate([blk_b(bbr), blk_b(bbi)], axis=2)
    cblk = jnp.concatenate([blk_c(c_re), -blk_c(c_im)], axis=1)
    return (bblk.astype(BF16), ar.reshape(1, S_LANES), ai.reshape(1, S_LANES), cblk.astype(BF16))


def _odd_layer(x, norm, w_in_all, ssm, d_skip, w_out_all, layer):
    bblk, ar, ai, cblk = ssm
    spec = pl.BlockSpec((BATCH, SCAN_STEPS, D_MODEL), lambda i: (0, i, 0))
    hbm = pl.BlockSpec(memory_space=pl.ANY)
    return pl.pallas_call(
        functools.partial(_odd_kernel, layer=layer),
        grid=(SEQ // SCAN_STEPS,),
        in_specs=[
            spec,
            _resident((1, D_MODEL)),
            hbm,
            _resident((S5_SPLIT, S5_CH, 2 * S5_ST)),
            _resident((1, S_LANES)),
            _resident((1, S_LANES)),
            _resident((S5_SPLIT, 2 * S5_ST, S5_CH)),
            _resident((1, C_WIDTH)),
            hbm,
        ],
        out_specs=spec,
        out_shape=jax.ShapeDtypeStruct((BATCH, SEQ, D_MODEL), F32),
        scratch_shapes=[pltpu.VMEM((SCAN_STEPS * BATCH, 2 * S_LANES), F32),
                        pltpu.VMEM((BATCH, 2 * S_LANES), F32),
                        pltpu.VMEM((D_MODEL, C_WIDTH), BF16),
                        pltpu.VMEM((C_WIDTH, 2 * D_MODEL), BF16),
                        _stage(CAST_ROWS, C_WIDTH),
                        _stage(CAST_ROWS_WIDE, 2 * D_MODEL),
                        pltpu.SemaphoreType.DMA((CAST_SLOTS,))],
        compiler_params=_params(1),
        name="odd_mixer",
    )(x, norm.reshape(1, -1), w_in_all, bblk, ar, ai, cblk, d_skip.reshape(1, -1), w_out_all)


def _kv_kernel(m_ref, g_ref, wk_hbm, wv_hbm, kt_ref, v_ref, wk_ref, wv_ref, stage, sem, *, layer):
    @pl.when(pl.program_id(0) == 0)
    def _():
        _load_bf16(wk_hbm.at[layer], wk_ref, stage, sem)
        _load_bf16(wv_hbm.at[layer], wv_ref, stage, sem)

    mn = _rms(m_ref[...], g_ref[...]).astype(BF16)
    kt_ref[...] = _mm(mn, wk_ref[...]).T.astype(BF16)
    v_ref[...] = _mm(mn, wv_ref[...]).astype(BF16)


def _square_weight_scratch(n):
    return ([pltpu.VMEM((D_MODEL, D_MODEL), BF16)] * n
            + [_stage(CAST_ROWS, D_MODEL), pltpu.SemaphoreType.DMA((CAST_SLOTS,))])


def _kv_proj(mem, norm, wk_all, wv_all, layer):
    spec = pl.BlockSpec((None, MEM_LEN, D_MODEL), lambda b: (b, 0, 0))
    spec_t = pl.BlockSpec((None, D_MODEL, MEM_LEN), lambda b: (b, 0, 0))
    hbm = pl.BlockSpec(memory_space=pl.ANY)
    return pl.pallas_call(
        functools.partial(_kv_kernel, layer=layer),
        grid=(BATCH,),
        in_specs=[spec, _resident((1, D_MODEL)), hbm, hbm],
        out_specs=[spec_t, spec],
        out_shape=[jax.ShapeDtypeStruct((BATCH, D_MODEL, MEM_LEN), BF16),
                   jax.ShapeDtypeStruct((BATCH, MEM_LEN, D_MODEL), BF16)],
        scratch_shapes=_square_weight_scratch(2),
        compiler_params=_params(1),
        name="ca_kv_proj",
    )(mem, norm.reshape(1, -1), wk_all, wv_all)


def _ca_kernel(x_ref, g_ref, wq_hbm, kt_ref, v_ref, wo_hbm, o_ref, wq_ref, wo_ref, stage, sem,
               *, layer):
    @pl.when((pl.program_id(0) == 0) & (pl.program_id(1) == 0))
    def _():
        _load_bf16(wq_hbm.at[layer], wq_ref, stage, sem)
        _load_bf16(wo_hbm.at[layer], wo_ref, stage, sem)

    rows = CA_TILE // CA_SPLIT
    for blk in range(CA_SPLIT):
        rs = slice(blk * rows, (blk + 1) * rows)
        x = x_ref[rs, :]
        q = _mm(_rms(x, g_ref[...]).astype(BF16), wq_ref[...]).astype(BF16)
        heads = []
        for h in range(CA_HEADS):
            hs = slice(h * CA_HEAD_DIM, (h + 1) * CA_HEAD_DIM)
            sc = _mm(q[:, hs], kt_ref[hs, :]) * (CA_HEAD_DIM ** -0.5)
            e = jnp.exp(sc - jnp.max(sc, axis=-1, keepdims=True))
            heads.append(_mm(e.astype(BF16), v_ref[:, hs]) / jnp.sum(e, axis=-1, keepdims=True))
        o = jnp.concatenate(heads, axis=1).astype(BF16)
        o_ref[rs, :] = x + _mm(o, wo_ref[...])


def _cross_attention(x, kt, v, norm, wq_all, wo_all, layer):
    kt_spec = pl.BlockSpec((None, D_MODEL, MEM_LEN), lambda b, s: (b, 0, 0))
    v_spec = pl.BlockSpec((None, MEM_LEN, D_MODEL), lambda b, s: (b, 0, 0))
    hbm = pl.BlockSpec(memory_space=pl.ANY)
    return pl.pallas_call(
        functools.partial(_ca_kernel, layer=layer),
        grid=(BATCH, SEQ // CA_TILE),
        in_specs=[_tok_spec(CA_TILE), _resident((1, D_MODEL)), hbm, kt_spec, v_spec, hbm],
        out_specs=_tok_spec(CA_TILE),
        out_shape=jax.ShapeDtypeStruct((BATCH, SEQ, D_MODEL), F32),
        scratch_shapes=_square_weight_scratch(2),
        compiler_params=_params(2),
        name="cross_attention",
    )(x, norm.reshape(1, -1), wq_all, kt, v, wo_all)


def _ffn_kernel(x_ref, g_ref, wg_hbm, wu_hbm, wd_hbm, fg_ref, o_ref,
                wg_ref, wu_ref, wd_ref, stage_in, stage_out, sem, *, layer, final):
    @pl.when((pl.program_id(0) == 0) & (pl.program_id(1) == 0))
    def _():
        _load_bf16(wg_hbm.at[layer], wg_ref, stage_in, sem)
        _load_bf16(wu_hbm.at[layer], wu_ref, stage_in, sem)
        _load_bf16(wd_hbm.at[layer], wd_ref, stage_out, sem)

    rows = FFN_TILE // FFN_SPLIT
    for blk in range(FFN_SPLIT):
        rs = slice(blk * rows, (blk + 1) * rows)
        x = x_ref[rs, :]
        xn = _rms(x, g_ref[...]).astype(BF16)
        y = x
        lo = 0
        for tiles in FFN_CHUNK_TILES:
            cs = slice(lo, lo + tiles * MXU_TILE)
            lo += tiles * MXU_TILE
            h = jax.nn.silu(_mm(xn, wg_ref[:, cs])) * _mm(xn, wu_ref[:, cs])
            y = y + _mm(h.astype(BF16), wd_ref[cs, :])
        if final:
            y = _rms(y, fg_ref[...])
        o_ref[rs, :] = y


def _ffn(x, norm, wg_all, wu_all, wd_all, final_norm, layer, final):
    hbm = pl.BlockSpec(memory_space=pl.ANY)
    return pl.pallas_call(
        functools.partial(_ffn_kernel, layer=layer, final=final),
        grid=(BATCH, SEQ // FFN_TILE),
        in_specs=[_tok_spec(FFN_TILE), _resident((1, D_MODEL)), hbm, hbm, hbm,
                  _resident((1, D_MODEL))],
        out_specs=_tok_spec(FFN_TILE),
        out_shape=jax.ShapeDtypeStruct((BATCH, SEQ, D_MODEL), F32),
        scratch_shapes=[pltpu.VMEM((D_MODEL, FFN_HIDDEN), BF16),
                        pltpu.VMEM((D_MODEL, FFN_HIDDEN), BF16),
                        pltpu.VMEM((FFN_HIDDEN, D_MODEL), BF16),
                        _stage(CAST_ROWS_WIDE, FFN_HIDDEN),
                        _stage(CAST_ROWS, D_MODEL),
                        pltpu.SemaphoreType.DMA((CAST_SLOTS,))],
        compiler_params=_params(2),
        name="ffn_final" if final else "ffn",
    )(x, norm.reshape(1, -1), wg_all, wu_all, wd_all, final_norm.reshape(1, -1))


def kernel(x, mem, e_norm, e_w_in, e_gmlp_w, e_gmlp_b, e_conv_w, e_conv_b, e_conv_ln_g, e_conv_ln_b, e_w_out, o_norm, o_w_in, o_lam_re, o_lam_im, o_log_dt, o_b_re, o_b_im, o_c_re, o_c_im, o_d, o_w_out, ca_norm, ca_mem_norm, ca_wq, ca_wk, ca_wv, ca_wo, ffn_norm, ffn_w_gate, ffn_w_up, ffn_w_down, final_norm):
    for i in range(DEPTH):
        j = i // 2
        if i % 2 == 0:
            x = _even_layer(x, e_norm[j], e_w_in, e_gmlp_w[j], e_gmlp_b[j], e_conv_w[j],
                            e_conv_b[j], e_conv_ln_g[j], e_conv_ln_b[j], e_w_out, layer=j)
        else:
            ssm = _s5_discretize(o_lam_re[j], o_lam_im[j], o_log_dt[j], o_b_re[j], o_b_im[j],
                                 o_c_re[j], o_c_im[j])
            x = _odd_layer(x, o_norm[j], o_w_in, ssm, o_d[j], o_w_out, layer=j)
        kt, v = _kv_proj(mem, ca_mem_norm[i], ca_wk, ca_wv, layer=i)
        x = _cross_attention(x, kt, v, ca_norm[i], ca_wq, ca_wo, layer=i)
        x = _ffn(x, ffn_norm[i], ffn_w_gate, ffn_w_up, ffn_w_down, final_norm,
                 layer=i, final=(i == DEPTH - 1))
    return x
```

```python
import functools
import math

import jax
import jax.numpy as jnp
from jax import lax
from jax.experimental import pallas as pl
from jax.experimental.pallas import tpu as pltpu

F32 = jnp.float32
BF16 = jnp.bfloat16

D_MODEL = 1024
BATCH = 8
SEQ = 4096
DEPTH = 2
CHUNK = 64
MEM_LEN = 256
EPS = 1e-6
A_WIDTH = 512
A_GROUPS = 4
A_GROUP_DIM = A_WIDTH // A_GROUPS
GMLP_BLOCK = 128
B_WIDTH = 512
CONV_WIDTH = 31
MIX_WIDTH = A_WIDTH + B_WIDTH
IN_WIDTH = 2 * A_WIDTH + 2 * B_WIDTH
C_WIDTH = 512
C_GROUP_CH = 16
C_GROUPS = C_WIDTH // C_GROUP_CH
C_STATE = 64
S_LANES = C_GROUPS * C_STATE
S5_SPLIT = 2
S5_CH = C_WIDTH // S5_SPLIT
S5_ST = S_LANES // S5_SPLIT
CA_HEADS = 4
CA_HEAD_DIM = D_MODEL // CA_HEADS
FFN_HIDDEN = -(-8 * D_MODEL // (3 * 256)) * 256

VMEM_LIMIT_BYTES = 56 * 1024 * 1024
SUBLANES = 8
LANES = 128
MXU_TILE = 256

EVEN_TILE = 512
CA_TILE = 2048
FFN_SPLIT = 2
CA_SPLIT = 4
FFN_TILE = 1024
CONV_HALO = 32
CONV_ROWS = 128
SCAN_STEPS = 64
SCAN_LANES = 1024
SCAN_UNROLL = 4
CAST_SLOTS = 3
CAST_ROWS = 256
CAST_ROWS_WIDE = 128
FFN_CHUNK_TILES = (11,)


def _params(n_axes):
    return pltpu.CompilerParams(
        dimension_semantics=("arbitrary",) * n_axes,
        vmem_limit_bytes=VMEM_LIMIT_BYTES)


def _resident(shape):
    return pl.BlockSpec(shape, lambda *_: (0,) * len(shape),
                        pipeline_mode=pl.Buffered(1))


def _tok_spec(tile):
    return pl.BlockSpec((None, tile, D_MODEL), lambda b, s: (b, s, 0))


def _rms(x, g):
    return x * lax.rsqrt(jnp.mean(x * x, axis=-1, keepdims=True) + EPS) * g


def _ln(x):
    mu = jnp.mean(x, axis=-1, keepdims=True)
    xc = x - mu
    return xc * lax.rsqrt(jnp.mean(xc * xc, axis=-1, keepdims=True) + EPS)


def _mm(a, b):
    return jnp.dot(a, b, preferred_element_type=F32)


def _load_bf16(src, dst_ref, stage_ref, sem_ref, scale=None):
    slots, chunk = stage_ref.shape[0], stage_ref.shape[1]
    n = dst_ref.shape[0] // chunk

    def copy(j):
        return pltpu.make_async_copy(src.at[pl.ds(j * chunk, chunk), :],
                                     stage_ref.at[j % slots], sem_ref.at[j % slots])

    for j in range(min(slots - 1, n)):
        copy(j).start()
    for j in range(n):
        if j + slots - 1 < n:
            copy(j + slots - 1).start()
        copy(j).wait()
        w = stage_ref[j % slots]
        dst_ref[j * chunk:(j + 1) * chunk, :] = (w if scale is None else w * scale).astype(BF16)


def _stage(chunk_rows, cols):
    return pltpu.VMEM((CAST_SLOTS, chunk_rows, cols), F32)


def _even_kernel(x_ref, g_ref, win_hbm, gw_ref, gb_ref, cw_ref, cb_ref, lng_ref,
                 lnb_ref, wout_hbm, o_ref, hext_ref, conv_ref, win_ref, wout_ref,
                 stage_in, stage_out, sem, *, layer):
    tm = EVEN_TILE
    nblk = tm // GMLP_BLOCK

    @pl.when((pl.program_id(0) == 0) & (pl.program_id(1) == 0))
    def _():
        _load_bf16(win_hbm.at[layer], win_ref, stage_in, sem)
        _load_bf16(wout_hbm.at[layer], wout_ref, stage_out, sem)

    x = x_ref[...]
    hn = _rms(x, g_ref[...]).astype(BF16)
    proj = _mm(hn, win_ref[...])

    u = jax.nn.gelu(proj[:, :A_WIDTH])
    v = _ln(jax.nn.gelu(proj[:, A_WIDTH:2 * A_WIDTH])).astype(BF16)
    ri = lax.broadcasted_iota(jnp.int32, (GMLP_BLOCK, GMLP_BLOCK), 0) >> 6
    ci = lax.broadcasted_iota(jnp.int32, (GMLP_BLOCK, GMLP_BLOCK), 1) >> 6
    keep = ci <= ri
    per_group = []
    for g in range(A_GROUPS):
        gs = slice(g * A_GROUP_DIM, (g + 1) * A_GROUP_DIM)
        rhs = jnp.concatenate(
            [v[nb * GMLP_BLOCK:(nb + 1) * GMLP_BLOCK, gs] for nb in range(nblk)], axis=1)
        wm = jnp.where(keep, gw_ref[g], 0.0).astype(BF16)
        per_group.append(_mm(wm, rhs))
    sg = jnp.concatenate(
        [jnp.concatenate([per_group[g][:, nb * A_GROUP_DIM:(nb + 1) * A_GROUP_DIM]
                          for g in range(A_GROUPS)], axis=1) for nb in range(nblk)], axis=0)
    bias = jnp.concatenate([gb_ref[...]] * nblk, axis=0)
    out_a = u * (sg + bias)

    h = proj[:, 2 * A_WIDTH:2 * A_WIDTH + B_WIDTH] * jax.nn.sigmoid(proj[:, 2 * A_WIDTH + B_WIDTH:])

    @pl.when(pl.program_id(1) == 0)
    def _():
        hext_ref[0:CONV_HALO, :] = jnp.zeros((CONV_HALO, B_WIDTH), F32)

    hext_ref[CONV_HALO:CONV_HALO + tm, :] = h
    for cb in range(B_WIDTH // LANES):
        cs = slice(cb * LANES, (cb + 1) * LANES)
        for rc in range(tm // CONV_ROWS):
            r0 = rc * CONV_ROWS
            win = hext_ref[r0:r0 + CONV_ROWS + CONV_HALO, cs]
            acc = jnp.zeros((CONV_ROWS, LANES), F32)
            for r in range(SUBLANES):
                shifted = win if r == 0 else pltpu.roll(win, r, axis=0)
                for q in range(CONV_HALO // SUBLANES):
                    lag = SUBLANES * q + r
                    if lag < CONV_WIDTH:
                        k = CONV_WIDTH - 1 - lag
                        lo = CONV_HALO - SUBLANES * q
                        acc = acc + cw_ref[k:k + 1, cs] * shifted[lo:lo + CONV_ROWS, :]
            conv_ref[r0:r0 + CONV_ROWS, cs] = acc
    hext_ref[0:CONV_HALO, :] = hext_ref[tm:tm + CONV_HALO, :]
    c = _ln(conv_ref[...] + cb_ref[...]) * lng_ref[...] + lnb_ref[...]
    out_b = jax.nn.silu(c)

    mix = (_mm(out_a.astype(BF16), wout_ref[0:A_WIDTH, :])
           + _mm(out_b.astype(BF16), wout_ref[A_WIDTH:MIX_WIDTH, :]))
    o_ref[...] = x + mix


def _even_layer(x, norm, w_in_all, gmlp_w, gmlp_b, conv_w, conv_b, ln_g, ln_b, w_out_all, layer):
    gb_full = jnp.repeat(gmlp_b.T, A_GROUP_DIM, axis=1)
    row = lambda a: a.reshape(1, -1)
    hbm = pl.BlockSpec(memory_space=pl.ANY)
    return pl.pallas_call(
        functools.partial(_even_kernel, layer=layer),
        grid=(BATCH, SEQ // EVEN_TILE),
        in_specs=[
            _tok_spec(EVEN_TILE),
            _resident((1, D_MODEL)),
            hbm,
            _resident((A_GROUPS, GMLP_BLOCK, GMLP_BLOCK)),
            _resident((GMLP_BLOCK, A_WIDTH)),
            _resident((CONV_WIDTH, B_WIDTH)),
            _resident((1, B_WIDTH)),
            _resident((1, B_WIDTH)),
            _resident((1, B_WIDTH)),
            hbm,
        ],
        out_specs=_tok_spec(EVEN_TILE),
        out_shape=jax.ShapeDtypeStruct((BATCH, SEQ, D_MODEL), F32),
        scratch_shapes=[pltpu.VMEM((CONV_HALO + EVEN_TILE, B_WIDTH), F32),
                        pltpu.VMEM((EVEN_TILE, B_WIDTH), F32),
                        pltpu.VMEM((D_MODEL, IN_WIDTH), BF16),
                        pltpu.VMEM((MIX_WIDTH, D_MODEL), BF16),
                        _stage(CAST_ROWS_WIDE, IN_WIDTH),
                        _stage(CAST_ROWS, D_MODEL),
                        pltpu.SemaphoreType.DMA((CAST_SLOTS,))],
        compiler_params=_params(2),
        name="even_mixer",
    )(x, row(norm), w_in_all, gmlp_w, gb_full, conv_w, row(conv_b), row(ln_g),
      row(ln_b), w_out_all)


def _odd_kernel(x_ref, g_ref, win_hbm, bblk_ref, ar_ref, ai_ref, cblk_ref, d_ref, wout_hbm,
                o_ref, xs_ref, st_ref, win_ref, wout_ref, stage_in, stage_out, sem, *, layer):
    steps = SCAN_STEPS
    rows = steps * BATCH

    @pl.when(pl.program_id(0) == 0)
    def _():
        st_ref[...] = jnp.zeros(st_ref.shape, F32)
        _load_bf16(win_hbm.at[layer], win_ref, stage_in, sem)
        _load_bf16(wout_hbm.at[layer], wout_ref, stage_out, sem)

    x = x_ref[...].reshape(rows, D_MODEL)
    u_bt = _mm(_rms(x, g_ref[...]).astype(BF16), win_ref[...])
    u = jnp.swapaxes(u_bt.reshape(BATCH, steps, C_WIDTH), 0, 1).reshape(rows, C_WIDTH)
    ub = u.astype(BF16)
    for m in range(S5_SPLIT):
        bu = _mm(ub[:, m * S5_CH:(m + 1) * S5_CH], bblk_ref[m])
        xs_ref[:, m * S5_ST:(m + 1) * S5_ST] = bu[:, :S5_ST]
        xs_ref[:, S_LANES + m * S5_ST:S_LANES + (m + 1) * S5_ST] = bu[:, S5_ST:]

    for lb in range(S_LANES // SCAN_LANES):
        re = slice(lb * SCAN_LANES, (lb + 1) * SCAN_LANES)
        im = slice(S_LANES + lb * SCAN_LANES, S_LANES + (lb + 1) * SCAN_LANES)
        ar = jnp.broadcast_to(ar_ref[:, re], (BATCH, SCAN_LANES))
        ai = jnp.broadcast_to(ai_ref[:, re], (BATCH, SCAN_LANES))

        def step(t, carry):
            sr, si = carry
            r = pl.ds(pl.multiple_of(t * BATCH, BATCH), BATCH)
            nr = ar * sr - ai * si + xs_ref[r, re]
            ni = ar * si + ai * sr + xs_ref[r, im]
            xs_ref[r, re] = nr
            xs_ref[r, im] = ni
            return nr, ni

        sr, si = lax.fori_loop(0, steps, step, (st_ref[:, re], st_ref[:, im]),
                               unroll=SCAN_UNROLL)
        st_ref[:, re] = sr
        st_ref[:, im] = si

    ys = []
    for m in range(S5_SPLIT):
        xr = xs_ref[:, m * S5_ST:(m + 1) * S5_ST].astype(BF16)
        xi = xs_ref[:, S_LANES + m * S5_ST:S_LANES + (m + 1) * S5_ST].astype(BF16)
        ys.append(_mm(xr, cblk_ref[m, 0:S5_ST, :]) + _mm(xi, cblk_ref[m, S5_ST:2 * S5_ST, :]))
    y = jnp.concatenate(ys, axis=1) + d_ref[...] * u
    y_bt = jnp.swapaxes(y.reshape(steps, BATCH, C_WIDTH), 0, 1).reshape(rows, C_WIDTH)
    o = _mm(jax.nn.gelu(y_bt).astype(BF16), wout_ref[...])
    out = x + o[:, :D_MODEL] * jax.nn.sigmoid(o[:, D_MODEL:])
    o_ref[...] = out.reshape(BATCH, steps, D_MODEL)


def _s5_discretize(lam_re, lam_im, log_dt, b_re, b_im, c_re, c_im):
    dt = jnp.exp(log_dt)[:, None]
    mag = jnp.exp(lam_re * dt)
    ar = mag * jnp.cos(lam_im * dt)
    ai = mag * jnp.sin(lam_im * dt)
    den = lam_re * lam_re + lam_im * lam_im
    qr = ((ar - 1.0) * lam_re + ai * lam_im) / den
    qi = (ai * lam_re - (ar - 1.0) * lam_im) / den
    bbr = qr[..., None] * b_re - qi[..., None] * b_im
    bbi = qr[..., None] * b_im + qi[..., None] * b_re
    gs = C_GROUPS // S5_SPLIT
    eye = jnp.eye(gs, dtype=F32)
    split = lambda a: a.reshape((S5_SPLIT, gs) + a.shape[1:])
    blk_b = lambda a: jnp.einsum('mgpc,gh->mgchp', split(a), eye).reshape(S5_SPLIT, S5_CH, S5_ST)
    blk_c = lambda a: jnp.einsum('mgcp,gh->mgphc', split(a), eye).reshape(S5_SPLIT, S5_ST, S5_CH)
    bblk = jnp.concatenate([blk_b(bbr), blk_b(bbi)], axis=2)
    cblk = jnp.concatenate([blk_c(c_re), -blk_c(c_im)], axis=1)
    return (bblk.astype(BF16), ar.reshape(1, S_LANES), ai.reshape(1, S_LANES), cblk.astype(BF16))


def _odd_layer(x, norm, w_in_all, ssm, d_skip, w_out_all, layer):
    bblk, ar, ai, cblk = ssm
    spec = pl.BlockSpec((BATCH, SCAN_STEPS, D_MODEL), lambda i: (0, i, 0))
    hbm = pl.BlockSpec(memory_space=pl.ANY)
    return pl.pallas_call(
        functools.partial(_odd_kernel, layer=layer),
        grid=(SEQ // SCAN_STEPS,),
        in_specs=[
            spec,
            _resident((1, D_MODEL)),
            hbm,
            _resident((S5_SPLIT, S5_CH, 2 * S5_ST)),
            _resident((1, S_LANES)),
            _resident((1, S_LANES)),
            _resident((S5_SPLIT, 2 * S5_ST, S5_CH)),
            _resident((1, C_WIDTH)),
            hbm,
        ],
        out_specs=spec,
        out_shape=jax.ShapeDtypeStruct((BATCH, SEQ, D_MODEL), F32),
        scratch_shapes=[pltpu.VMEM((SCAN_STEPS * BATCH, 2 * S_LANES), F32),
                        pltpu.VMEM((BATCH, 2 * S_LANES), F32),
                        pltpu.VMEM((D_MODEL, C_WIDTH), BF16),
                        pltpu.VMEM((C_WIDTH, 2 * D_MODEL), BF16),
                        _stage(CAST_ROWS, C_WIDTH),
                        _stage(CAST_ROWS_WIDE, 2 * D_MODEL),
                        pltpu.SemaphoreType.DMA((CAST_SLOTS,))],
        compiler_params=_params(1),
        name="odd_mixer",
    )(x, norm.reshape(1, -1), w_in_all, bblk, ar, ai, cblk, d_skip.reshape(1, -1), w_out_all)


def _ca_kernel(x_ref, m_ref, g_ref, gm_ref, wq_hbm, wk_hbm, wv_hbm, wo_hbm, o_ref,
               wq_ref, wk_ref, wv_ref, wo_ref, kt_ref, v_ref, stage, sem, *, layer):
    @pl.when((pl.program_id(0) == 0) & (pl.program_id(1) == 0))
    def _():
        _load_bf16(wq_hbm.at[layer], wq_ref, stage, sem, scale=CA_HEAD_DIM ** -0.5)
        _load_bf16(wk_hbm.at[layer], wk_ref, stage, sem)
        _load_bf16(wv_hbm.at[layer], wv_ref, stage, sem)
        _load_bf16(wo_hbm.at[layer], wo_ref, stage, sem)

    @pl.when(pl.program_id(1) == 0)
    def _():
        mn = _rms(m_ref[...], gm_ref[...]).astype(BF16)
        kt_ref[...] = _mm(mn, wk_ref[...]).T.astype(BF16)
        v_ref[...] = _mm(mn, wv_ref[...]).astype(BF16)

    rows = CA_TILE // CA_SPLIT
    for blk in range(CA_SPLIT):
        rs = slice(blk * rows, (blk + 1) * rows)
        x = x_ref[rs, :]
        q = _mm(_rms(x, g_ref[...]).astype(BF16), wq_ref[...]).astype(BF16)
        heads = []
        for h in range(CA_HEADS):
            hs = slice(h * CA_HEAD_DIM, (h + 1) * CA_HEAD_DIM)
            sc = _mm(q[:, hs], kt_ref[hs, :])
            e = jnp.exp(sc - jnp.max(sc, axis=-1, keepdims=True))
            heads.append(_mm(e.astype(BF16), v_ref[:, hs]) / jnp.sum(e, axis=-1, keepdims=True))
        o = jnp.concatenate(heads, axis=1).astype(BF16)
        o_ref[rs, :] = x + _mm(o, wo_ref[...])


def _cross_attention(x, mem, norm, mem_norm, wq_all, wk_all, wv_all, wo_all, layer):
    mem_spec = pl.BlockSpec((None, MEM_LEN, D_MODEL), lambda b, s: (b, 0, 0))
    hbm = pl.BlockSpec(memory_space=pl.ANY)
    return pl.pallas_call(
        functools.partial(_ca_kernel, layer=layer),
        grid=(BATCH, SEQ // CA_TILE),
        in_specs=[_tok_spec(CA_TILE), mem_spec, _resident((1, D_MODEL)), _resident((1, D_MODEL)),
                  hbm, hbm, hbm, hbm],
        out_specs=_tok_spec(CA_TILE),
        out_shape=jax.ShapeDtypeStruct((BATCH, SEQ, D_MODEL), F32),
        scratch_shapes=([pltpu.VMEM((D_MODEL, D_MODEL), BF16)] * 4
                        + [pltpu.VMEM((D_MODEL, MEM_LEN), BF16),
                           pltpu.VMEM((MEM_LEN, D_MODEL), BF16),
                           _stage(CAST_ROWS, D_MODEL),
                           pltpu.SemaphoreType.DMA((CAST_SLOTS,))]),
        compiler_params=_params(2),
        name="cross_attention",
    )(x, mem, norm.reshape(1, -1), mem_norm.reshape(1, -1), wq_all, wk_all, wv_all, wo_all)


def _ffn_kernel(x_ref, g_ref, wg_hbm, wu_hbm, wd_hbm, fg_ref, o_ref,
                wg_ref, wu_ref, wd_ref, stage_in, stage_out, sem, *, layer, final):
    @pl.when((pl.program_id(0) == 0) & (pl.program_id(1) == 0))
    def _():
        _load_bf16(wg_hbm.at[layer], wg_ref, stage_in, sem)
        _load_bf16(wu_hbm.at[layer], wu_ref, stage_in, sem)
        _load_bf16(wd_hbm.at[layer], wd_ref, stage_out, sem)

    rows = FFN_TILE // FFN_SPLIT
    for blk in range(FFN_SPLIT):
        rs = slice(blk * rows, (blk + 1) * rows)
        x = x_ref[rs, :]
        xn = _rms(x, g_ref[...]).astype(BF16)
        y = x
        lo = 0
        for tiles in FFN_CHUNK_TILES:
            cs = slice(lo, lo + tiles * MXU_TILE)
            lo += tiles * MXU_TILE
            h = jax.nn.silu(_mm(xn, wg_ref[:, cs])) * _mm(xn, wu_ref[:, cs])
            y = y + _mm(h.astype(BF16), wd_ref[cs, :])
        if final:
            y = _rms(y, fg_ref[...])
        o_ref[rs, :] = y


def _ffn(x, norm, wg_all, wu_all, wd_all, final_norm, layer, final):
    hbm = pl.BlockSpec(memory_space=pl.ANY)
    return pl.pallas_call(
        functools.partial(_ffn_kernel, layer=layer, final=final),
        grid=(BATCH, SEQ // FFN_TILE),
        in_specs=[_tok_spec(FFN_TILE), _resident((1, D_MODEL)), hbm, hbm, hbm,
                  _resident((1, D_MODEL))],
        out_specs=_tok_spec(FFN_TILE),
        out_shape=jax.ShapeDtypeStruct((BATCH, SEQ, D_MODEL), F32),
        scratch_shapes=[pltpu.VMEM((D_MODEL, FFN_HIDDEN), BF16),
                        pltpu.VMEM((D_MODEL, FFN_HIDDEN), BF16),
                        pltpu.VMEM((FFN_HIDDEN, D_MODEL), BF16),
                        _stage(CAST_ROWS_WIDE, FFN_HIDDEN),
                        _stage(CAST_ROWS, D_MODEL),
                        pltpu.SemaphoreType.DMA((CAST_SLOTS,))],
        compiler_params=_params(2),
        name="ffn_final" if final else "ffn",
    )(x, norm.reshape(1, -1), wg_all, wu_all, wd_all, final_norm.reshape(1, -1))


def kernel(x, mem, e_norm, e_w_in, e_gmlp_w, e_gmlp_b, e_conv_w, e_conv_b, e_conv_ln_g, e_conv_ln_b, e_w_out, o_norm, o_w_in, o_lam_re, o_lam_im, o_log_dt, o_b_re, o_b_im, o_c_re, o_c_im, o_d, o_w_out, ca_norm, ca_mem_norm, ca_wq, ca_wk, ca_wv, ca_wo, ffn_norm, ffn_w_gate, ffn_w_up, ffn_w_down, final_norm):
    for i in range(DEPTH):
        j = i // 2
        if i % 2 == 0:
            x = _even_layer(x, e_norm[j], e_w_in, e_gmlp_w[j], e_gmlp_b[j], e_conv_w[j],
                            e_conv_b[j], e_conv_ln_g[j], e_conv_ln_b[j], e_w_out, layer=j)
        else:
            ssm = _s5_discretize(o_lam_re[j], o_lam_im[j], o_log_dt[j], o_b_re[j], o_b_im[j],
                                 o_c_re[j], o_c_im[j])
            x = _odd_layer(x, o_norm[j], o_w_in, ssm, o_d[j], o_w_out, layer=j)
        x = _cross_attention(x, mem, ca_norm[i], ca_mem_norm[i], ca_wq, ca_wk, ca_wv, ca_wo,
                             layer=i)
        x = _ffn(x, ffn_norm[i], ffn_w_gate, ffn_w_up, ffn_w_down, final_norm,
                 layer=i, final=(i == DEPTH - 1))
    return x
```

```python
import functools
import math

import jax
import jax.numpy as jnp
from jax import lax
from jax.experimental import pallas as pl
from jax.experimental.pallas import tpu as pltpu

F32 = jnp.float32
BF16 = jnp.bfloat16

D_MODEL = 1024
BATCH = 8
SEQ = 4096
DEPTH = 2
CHUNK = 64
MEM_LEN = 256
EPS = 1e-6
A_WIDTH = 512
A_GROUPS = 4
A_GROUP_DIM = A_WIDTH // A_GROUPS
GMLP_BLOCK = 128
B_WIDTH = 512
CONV_WIDTH = 31
MIX_WIDTH = A_WIDTH + B_WIDTH
IN_WIDTH = 2 * A_WIDTH + 2 * B_WIDTH
C_WIDTH = 512
C_GROUP_CH = 16
C_GROUPS = C_WIDTH // C_GROUP_CH
C_STATE = 64
S_LANES = C_GROUPS * C_STATE
S5_SPLIT = 2
S5_CH = C_WIDTH // S5_SPLIT
S5_ST = S_LANES // S5_SPLIT
CA_HEADS = 4
CA_HEAD_DIM = D_MODEL // CA_HEADS
FFN_HIDDEN = -(-8 * D_MODEL // (3 * 256)) * 256

VMEM_LIMIT_BYTES = 56 * 1024 * 1024
SUBLANES = 8
LANES = 128
MXU_TILE = 256

EVEN_TILE = 1024
CA_TILE = 2048
FFN_SPLIT = 2
CA_SPLIT = 4
FFN_TILE = 1024
CONV_HALO = 32
CONV_ROWS = 128
SCAN_STEPS = 128
SCAN_LANES = 1024
SCAN_UNROLL = 4
CAST_SLOTS = 3
CAST_ROWS = 256
CAST_ROWS_WIDE = 128
FFN_CHUNK_TILES = (11,)


def _params(n_axes):
    return pltpu.CompilerParams(
        dimension_semantics=("arbitrary",) * n_axes,
        vmem_limit_bytes=VMEM_LIMIT_BYTES)


def _resident(shape):
    return pl.BlockSpec(shape, lambda *_: (0,) * len(shape),
                        pipeline_mode=pl.Buffered(1))


def _tok_spec(tile):
    return pl.BlockSpec((None, tile, D_MODEL), lambda b, s: (b, s, 0))


def _rms(x, g):
    return x * lax.rsqrt(jnp.mean(x * x, axis=-1, keepdims=True) + EPS) * g


def _ln(x):
    mu = jnp.mean(x, axis=-1, keepdims=True)
    xc = x - mu
    return xc * lax.rsqrt(jnp.mean(xc * xc, axis=-1, keepdims=True) + EPS)


def _mm(a, b):
    return jnp.dot(a, b, preferred_element_type=F32)


def _load_bf16(src, dst_ref, stage_ref, sem_ref, scale=None):
    slots, chunk = stage_ref.shape[0], stage_ref.shape[1]
    n = dst_ref.shape[0] // chunk

    def copy(j):
        return pltpu.make_async_copy(src.at[pl.ds(j * chunk, chunk), :],
                                     stage_ref.at[j % slots], sem_ref.at[j % slots])

    for j in range(min(slots - 1, n)):
        copy(j).start()
    for j in range(n):
        if j + slots - 1 < n:
            copy(j + slots - 1).start()
        copy(j).wait()
        w = stage_ref[j % slots]
        dst_ref[j * chunk:(j + 1) * chunk, :] = (w if scale is None else w * scale).astype(BF16)


def _stage(chunk_rows, cols):
    return pltpu.VMEM((CAST_SLOTS, chunk_rows, cols), F32)


def _even_kernel(x_ref, g_ref, win_hbm, gw_ref, gb_ref, cw_ref, cb_ref, lng_ref,
                 lnb_ref, wout_hbm, o_ref, hext_ref, conv_ref, win_ref, wout_ref,
                 stage_in, stage_out, sem, *, layer):
    tm = EVEN_TILE
    nblk = tm // GMLP_BLOCK

    @pl.when((pl.program_id(0) == 0) & (pl.program_id(1) == 0))
    def _():
        _load_bf16(win_hbm.at[layer], win_ref, stage_in, sem)
        _load_bf16(wout_hbm.at[layer], wout_ref, stage_out, sem)

    x = x_ref[...]
    hn = _rms(x, g_ref[...]).astype(BF16)
    proj = _mm(hn, win_ref[...])

    u = jax.nn.gelu(proj[:, :A_WIDTH])
    v = _ln(jax.nn.gelu(proj[:, A_WIDTH:2 * A_WIDTH])).astype(BF16)
    ri = lax.broadcasted_iota(jnp.int32, (GMLP_BLOCK, GMLP_BLOCK), 0) >> 6
    ci = lax.broadcasted_iota(jnp.int32, (GMLP_BLOCK, GMLP_BLOCK), 1) >> 6
    keep = ci <= ri
    per_group = []
    for g in range(A_GROUPS):
        gs = slice(g * A_GROUP_DIM, (g + 1) * A_GROUP_DIM)
        rhs = jnp.concatenate(
            [v[nb * GMLP_BLOCK:(nb + 1) * GMLP_BLOCK, gs] for nb in range(nblk)], axis=1)
        wm = jnp.where(keep, gw_ref[g], 0.0).astype(BF16)
        per_group.append(_mm(wm, rhs))
    sg = jnp.concatenate(
        [jnp.concatenate([per_group[g][:, nb * A_GROUP_DIM:(nb + 1) * A_GROUP_DIM]
                          for g in range(A_GROUPS)], axis=1) for nb in range(nblk)], axis=0)
    bias = jnp.concatenate([gb_ref[...]] * nblk, axis=0)
    out_a = u * (sg + bias)

    h = proj[:, 2 * A_WIDTH:2 * A_WIDTH + B_WIDTH] * jax.nn.sigmoid(proj[:, 2 * A_WIDTH + B_WIDTH:])

    @pl.when(pl.program_id(1) == 0)
    def _():
        hext_ref[0:CONV_HALO, :] = jnp.zeros((CONV_HALO, B_WIDTH), F32)

    hext_ref[CONV_HALO:CONV_HALO + tm, :] = h
    for cb in range(B_WIDTH // LANES):
        cs = slice(cb * LANES, (cb + 1) * LANES)
        for rc in range(tm // CONV_ROWS):
            r0 = rc * CONV_ROWS
            win = hext_ref[r0:r0 + CONV_ROWS + CONV_HALO, cs]
            acc = jnp.zeros((CONV_ROWS, LANES), F32)
            for r in range(SUBLANES):
                shifted = win if r == 0 else pltpu.roll(win, r, axis=0)
                for q in range(CONV_HALO // SUBLANES):
                    lag = SUBLANES * q + r
                    if lag < CONV_WIDTH:
                        k = CONV_WIDTH - 1 - lag
                        lo = CONV_HALO - SUBLANES * q
                        acc = acc + cw_ref[k:k + 1, cs] * shifted[lo:lo + CONV_ROWS, :]
            conv_ref[r0:r0 + CONV_ROWS, cs] = acc
    hext_ref[0:CONV_HALO, :] = hext_ref[tm:tm + CONV_HALO, :]
    c = _ln(conv_ref[...] + cb_ref[...]) * lng_ref[...] + lnb_ref[...]
    out_b = jax.nn.silu(c)

    mix = (_mm(out_a.astype(BF16), wout_ref[0:A_WIDTH, :])
           + _mm(out_b.astype(BF16), wout_ref[A_WIDTH:MIX_WIDTH, :]))
    o_ref[...] = x + mix


def _even_layer(x, norm, w_in_all, gmlp_w, gmlp_b, conv_w, conv_b, ln_g, ln_b, w_out_all, layer):
    gb_full = jnp.repeat(gmlp_b.T, A_GROUP_DIM, axis=1)
    row = lambda a: a.reshape(1, -1)
    hbm = pl.BlockSpec(memory_space=pl.ANY)
    return pl.pallas_call(
        functools.partial(_even_kernel, layer=layer),
        grid=(BATCH, SEQ // EVEN_TILE),
        in_specs=[
            _tok_spec(EVEN_TILE),
            _resident((1, D_MODEL)),
            hbm,
            _resident((A_GROUPS, GMLP_BLOCK, GMLP_BLOCK)),
            _resident((GMLP_BLOCK, A_WIDTH)),
            _resident((CONV_WIDTH, B_WIDTH)),
            _resident((1, B_WIDTH)),
            _resident((1, B_WIDTH)),
            _resident((1, B_WIDTH)),
            hbm,
        ],
        out_specs=_tok_spec(EVEN_TILE),
        out_shape=jax.ShapeDtypeStruct((BATCH, SEQ, D_MODEL), F32),
        scratch_shapes=[pltpu.VMEM((CONV_HALO + EVEN_TILE, B_WIDTH), F32),
                        pltpu.VMEM((EVEN_TILE, B_WIDTH), F32),
                        pltpu.VMEM((D_MODEL, IN_WIDTH), BF16),
                        pltpu.VMEM((MIX_WIDTH, D_MODEL), BF16),
                        _stage(CAST_ROWS_WIDE, IN_WIDTH),
                        _stage(CAST_ROWS, D_MODEL),
                        pltpu.SemaphoreType.DMA((CAST_SLOTS,))],
        compiler_params=_params(2),
        name="even_mixer",
    )(x, row(norm), w_in_all, gmlp_w, gb_full, conv_w, row(conv_b), row(ln_g),
      row(ln_b), w_out_all)


def _odd_kernel(x_ref, g_ref, win_hbm, bblk_ref, ar_ref, ai_ref, cblk_ref, d_ref, wout_hbm,
                o_ref, xs_ref, st_ref, win_ref, wout_ref, stage_in, stage_out, sem, *, layer):
    steps = SCAN_STEPS
    rows = steps * BATCH

    @pl.when(pl.program_id(0) == 0)
    def _():
        st_ref[...] = jnp.zeros(st_ref.shape, F32)
        _load_bf16(win_hbm.at[layer], win_ref, stage_in, sem)
        _load_bf16(wout_hbm.at[layer], wout_ref, stage_out, sem)

    x = x_ref[...].reshape(rows, D_MODEL)
    u_bt = _mm(_rms(x, g_ref[...]).astype(BF16), win_ref[...])
    u = jnp.swapaxes(u_bt.reshape(BATCH, steps, C_WIDTH), 0, 1).reshape(rows, C_WIDTH)
    ub = u.astype(BF16)
    for m in range(S5_SPLIT):
        bu = _mm(ub[:, m * S5_CH:(m + 1) * S5_CH], bblk_ref[m])
        xs_ref[:, m * S5_ST:(m + 1) * S5_ST] = bu[:, :S5_ST]
        xs_ref[:, S_LANES + m * S5_ST:S_LANES + (m + 1) * S5_ST] = bu[:, S5_ST:]

    for lb in range(S_LANES // SCAN_LANES):
        re = slice(lb * SCAN_LANES, (lb + 1) * SCAN_LANES)
        im = slice(S_LANES + lb * SCAN_LANES, S_LANES + (lb + 1) * SCAN_LANES)
        ar = jnp.broadcast_to(ar_ref[:, re], (BATCH, SCAN_LANES))
        ai = jnp.broadcast_to(ai_ref[:, re], (BATCH, SCAN_LANES))

        def step(t, carry):
            sr, si = carry
            r = pl.ds(pl.multiple_of(t * BATCH, BATCH), BATCH)
            nr = ar * sr - ai * si + xs_ref[r, re]
            ni = ar * si + ai * sr + xs_ref[r, im]
            xs_ref[r, re] = nr
            xs_ref[r, im] = ni
            return nr, ni

        sr, si = lax.fori_loop(0, steps, step, (st_ref[:, re], st_ref[:, im]),
                               unroll=SCAN_UNROLL)
        st_ref[:, re] = sr
        st_ref[:, im] = si

    ys = []
    for m in range(S5_SPLIT):
        xr = xs_ref[:, m * S5_ST:(m + 1) * S5_ST].astype(BF16)
        xi = xs_ref[:, S_LANES + m * S5_ST:S_LANES + (m + 1) * S5_ST].astype(BF16)
        ys.append(_mm(xr, cblk_ref[m, 0:S5_ST, :]) + _mm(xi, cblk_ref[m, S5_ST:2 * S5_ST, :]))
    y = jnp.concatenate(ys, axis=1) + d_ref[...] * u
    y_bt = jnp.swapaxes(y.reshape(steps, BATCH, C_WIDTH), 0, 1).reshape(rows, C_WIDTH)
    o = _mm(jax.nn.gelu(y_bt).astype(BF16), wout_ref[...])
    out = x + o[:, :D_MODEL] * jax.nn.sigmoid(o[:, D_MODEL:])
    o_ref[...] = out.reshape(BATCH, steps, D_MODEL)


def _s5_discretize(lam_re, lam_im, log_dt, b_re, b_im, c_re, c_im):
    dt = jnp.exp(log_dt)[:, None]
    mag = jnp.exp(lam_re * dt)
    ar = mag * jnp.cos(lam_im * dt)
    ai = mag * jnp.sin(lam_im * dt)
    den = lam_re * lam_re + lam_im * lam_im
    qr = ((ar - 1.0) * lam_re + ai * lam_im) / den
    qi = (ai * lam_re - (ar - 1.0) * lam_im) / den
    bbr = qr[..., None] * b_re - qi[..., None] * b_im
    bbi = qr[..., None] * b_im + qi[..., None] * b_re
    gs = C_GROUPS // S5_SPLIT
    eye = jnp.eye(gs, dtype=F32)
    split = lambda a: a.reshape((S5_SPLIT, gs) + a.shape[1:])
    blk_b = lambda a: jnp.einsum('mgpc,gh->mgchp', split(a), eye).reshape(S5_SPLIT, S5_CH, S5_ST)
    blk_c = lambda a: jnp.einsum('mgcp,gh->mgphc', split(a), eye).reshape(S5_SPLIT, S5_ST, S5_CH)
    bblk = jnp.concatenate([blk_b(bbr), blk_b(bbi)], axis=2)
    cblk = jnp.concatenate([blk_c(c_re), -blk_c(c_im)], axis=1)
    return (bblk.astype(BF16), ar.reshape(1, S_LANES), ai.reshape(1, S_LANES), cblk.astype(BF16))


def _odd_layer(x, norm, w_in_all, ssm, d_skip, w_out_all, layer):
    bblk, ar, ai, cblk = ssm
    spec = pl.BlockSpec((BATCH, SCAN_STEPS, D_MODEL), lambda i: (0, i, 0))
    hbm = pl.BlockSpec(memory_space=pl.ANY)
    return pl.pallas_call(
        functools.partial(_odd_kernel, layer=layer),
        grid=(SEQ // SCAN_STEPS,),
        in_specs=[
            spec,
            _resident((1, D_MODEL)),
            hbm,
            _resident((S5_SPLIT, S5_CH, 2 * S5_ST)),
            _resident((1, S_LANES)),
            _resident((1, S_LANES)),
            _resident((S5_SPLIT, 2 * S5_ST, S5_CH)),
            _resident((1, C_WIDTH)),
            hbm,
        ],
        out_specs=spec,
        out_shape=jax.ShapeDtypeStruct((BATCH, SEQ, D_MODEL), F32),
        scratch_shapes=[pltpu.VMEM((SCAN_STEPS * BATCH, 2 * S_LANES), F32),
                        pltpu.VMEM((BATCH, 2 * S_LANES), F32),
                        pltpu.VMEM((D_MODEL, C_WIDTH), BF16),
                        pltpu.VMEM((C_WIDTH, 2 * D_MODEL), BF16),
                        _stage(CAST_ROWS, C_WIDTH),
                        _stage(CAST_ROWS_WIDE, 2 * D_MODEL),
                        pltpu.SemaphoreType.DMA((CAST_SLOTS,))],
        compiler_params=_params(1),
        name="odd_mixer",
    )(x, norm.reshape(1, -1), w_in_all, bblk, ar, ai, cblk, d_skip.reshape(1, -1), w_out_all)


def _ca_kernel(x_ref, m_ref, g_ref, gm_ref, wq_hbm, wk_hbm, wv_hbm, wo_hbm, o_ref,
               wq_ref, wk_ref, wv_ref, wo_ref, kt_ref, v_ref, stage, sem, *, layer):
    @pl.when((pl.program_id(0) == 0) & (pl.program_id(1) == 0))
    def _():
        _load_bf16(wq_hbm.at[layer], wq_ref, stage, sem, scale=CA_HEAD_DIM ** -0.5)
        _load_bf16(wk_hbm.at[layer], wk_ref, stage, sem)
        _load_bf16(wv_hbm.at[layer], wv_ref, stage, sem)
        _load_bf16(wo_hbm.at[layer], wo_ref, stage, sem)

    @pl.when(pl.program_id(1) == 0)
    def _():
        mn = _rms(m_ref[...], gm_ref[...]).astype(BF16)
        kt_ref[...] = _mm(mn, wk_ref[...]).T.astype(BF16)
        v_ref[...] = _mm(mn, wv_ref[...]).astype(BF16)

    rows = CA_TILE // CA_SPLIT
    for blk in range(CA_SPLIT):
        rs = slice(blk * rows, (blk + 1) * rows)
        x = x_ref[rs, :]
        q = _mm(_rms(x, g_ref[...]).astype(BF16), wq_ref[...]).astype(BF16)
        heads = []
        for h in range(CA_HEADS):
            hs = slice(h * CA_HEAD_DIM, (h + 1) * CA_HEAD_DIM)
            sc = _mm(q[:, hs], kt_ref[hs, :])
            e = jnp.exp(sc - jnp.max(sc, axis=-1, keepdims=True))
            heads.append(_mm(e.astype(BF16), v_ref[:, hs]) / jnp.sum(e, axis=-1, keepdims=True))
        o = jnp.concatenate(heads, axis=1).astype(BF16)
        o_ref[rs, :] = x + _mm(o, wo_ref[...])


def _cross_attention(x, mem, norm, mem_norm, wq_all, wk_all, wv_all, wo_all, layer):
    mem_spec = pl.BlockSpec((None, MEM_LEN, D_MODEL), lambda b, s: (b, 0, 0))
    hbm = pl.BlockSpec(memory_space=pl.ANY)
    return pl.pallas_call(
        functools.partial(_ca_kernel, layer=layer),
        grid=(BATCH, SEQ // CA_TILE),
        in_specs=[_tok_spec(CA_TILE), mem_spec, _resident((1, D_MODEL)), _resident((1, D_MODEL)),
                  hbm, hbm, hbm, hbm],
        out_specs=_tok_spec(CA_TILE),
        out_shape=jax.ShapeDtypeStruct((BATCH, SEQ, D_MODEL), F32),
        scratch_shapes=([pltpu.VMEM((D_MODEL, D_MODEL), BF16)] * 4
                        + [pltpu.VMEM((D_MODEL, MEM_LEN), BF16),
                           pltpu.VMEM((MEM_LEN, D_MODEL), BF16),
                           _stage(CAST_ROWS, D_MODEL),
                           pltpu.SemaphoreType.DMA((CAST_SLOTS,))]),
        compiler_params=_params(2),
        name="cross_attention",
    )(x, mem, norm.reshape(1, -1), mem_norm.reshape(1, -1), wq_all, wk_all, wv_all, wo_all)


def _ffn_kernel(x_ref, g_ref, wg_hbm, wu_hbm, wd_hbm, fg_ref, o_ref,
                wg_ref, wu_ref, wd_ref, stage_in, stage_out, sem, *, layer, final):
    @pl.when((pl.program_id(0) == 0) & (pl.program_id(1) == 0))
    def _():
        _load_bf16(wg_hbm.at[layer], wg_ref, stage_in, sem)
        _load_bf16(wu_hbm.at[layer], wu_ref, stage_in, sem)
        _load_bf16(wd_hbm.at[layer], wd_ref, stage_out, sem)

    rows = FFN_TILE // FFN_SPLIT
    for blk in range(FFN_SPLIT):
        rs = slice(blk * rows, (blk + 1) * rows)
        x = x_ref[rs, :]
        xn = _rms(x, g_ref[...]).astype(BF16)
        y = x
        lo = 0
        for tiles in FFN_CHUNK_TILES:
            cs = slice(lo, lo + tiles * MXU_TILE)
            lo += tiles * MXU_TILE
            h = jax.nn.silu(_mm(xn, wg_ref[:, cs])) * _mm(xn, wu_ref[:, cs])
            y = y + _mm(h.astype(BF16), wd_ref[cs, :])
        if final:
            y = _rms(y, fg_ref[...])
        o_ref[rs, :] = y


def _ffn(x, norm, wg_all, wu_all, wd_all, final_norm, layer, final):
    hbm = pl.BlockSpec(memory_space=pl.ANY)
    return pl.pallas_call(
        functools.partial(_ffn_kernel, layer=layer, final=final),
        grid=(BATCH, SEQ // FFN_TILE),
        in_specs=[_tok_spec(FFN_TILE), _resident((1, D_MODEL)), hbm, hbm, hbm,
                  _resident((1, D_MODEL))],
        out_specs=_tok_spec(FFN_TILE),
        out_shape=jax.ShapeDtypeStruct((BATCH, SEQ, D_MODEL), F32),
        scratch_shapes=[pltpu.VMEM((D_MODEL, FFN_HIDDEN), BF16),
                        pltpu.VMEM((D_MODEL, FFN_HIDDEN), BF16),
                        pltpu.VMEM((FFN_HIDDEN, D_MODEL), BF16),
                        _stage(CAST_ROWS_WIDE, FFN_HIDDEN),
                        _stage(CAST_ROWS, D_MODEL),
                        pltpu.SemaphoreType.DMA((CAST_SLOTS,))],
        compiler_params=_params(2),
        name="ffn_final" if final else "ffn",
    )(x, norm.reshape(1, -1), wg_all, wu_all, wd_all, final_norm.reshape(1, -1))


def kernel(x, mem, e_norm, e_w_in, e_gmlp_w, e_gmlp_b, e_conv_w, e_conv_b, e_conv_ln_g, e_conv_ln_b, e_w_out, o_norm, o_w_in, o_lam_re, o_lam_im, o_log_dt, o_b_re, o_b_im, o_c_re, o_c_im, o_d, o_w_out, ca_norm, ca_mem_norm, ca_wq, ca_wk, ca_wv, ca_wo, ffn_norm, ffn_w_gate, ffn_w_up, ffn_w_down, final_norm):
    for i in range(DEPTH):
        j = i // 2
        if i % 2 == 0:
            x = _even_layer(x, e_norm[j], e_w_in, e_gmlp_w[j], e_gmlp_b[j], e_conv_w[j],
                            e_conv_b[j], e_conv_ln_g[j], e_conv_ln_b[j], e_w_out, layer=j)
        else:
            ssm = _s5_discretize(o_lam_re[j], o_lam_im[j], o_log_dt[j], o_b_re[j], o_b_im[j],
                                 o_c_re[j], o_c_im[j])
            x = _odd_layer(x, o_norm[j], o_w_in, ssm, o_d[j], o_w_out, layer=j)
        x = _cross_attention(x, mem, ca_norm[i], ca_mem_norm[i], ca_wq, ca_wk, ca_wv, ca_wo,
                             layer=i)
        x = _ffn(x, ffn_norm[i], ffn_w_gate, ffn_w_up, ffn_w_down, final_norm,
                 layer=i, final=(i == DEPTH - 1))
    return x
```

```python
import functools

import jax
import jax.numpy as jnp
from jax import lax
from jax.experimental import pallas as pl
from jax.experimental.pallas import tpu as pltpu

F32 = jnp.float32
BF16 = jnp.bfloat16

D_MODEL = 1024
BATCH = 8
SEQ = 4096
DEPTH = 2
CHUNK = 64
CHUNK_SHIFT = CHUNK.bit_length() - 1
MEM_LEN = 256
EPS = 1e-6
A_WIDTH = 512
A_GROUPS = 4
A_GROUP_DIM = A_WIDTH // A_GROUPS
GMLP_BLOCK = 128
B_WIDTH = 512
CONV_WIDTH = 31
MIX_WIDTH = A_WIDTH + B_WIDTH
IN_WIDTH = 2 * A_WIDTH + 2 * B_WIDTH
C_WIDTH = 512
C_GROUP_CH = 16
C_GROUPS = C_WIDTH // C_GROUP_CH
C_STATE = 64
S_LANES = C_GROUPS * C_STATE
S5_SPLIT = 2
S5_CH = C_WIDTH // S5_SPLIT
S5_ST = S_LANES // S5_SPLIT
CA_HEADS = 4
CA_HEAD_DIM = D_MODEL // CA_HEADS
FFN_HIDDEN = -(-8 * D_MODEL // (3 * 256)) * 256

VMEM_LIMIT_BYTES = 56 * 1024 * 1024
SUBLANES = 8
LANES = 128
MXU_TILE = 256

EVEN_TILE = 1024
CA_TILE = 2048
CA_SPLIT = 4
FFN_TILE = 1024
FFN_SPLIT = 2
CONV_HALO = 32
CONV_ROWS = 128
SCAN_STEPS = 128
SCAN_LANES = 1024
SCAN_UNROLL = 4
CAST_SLOTS = 3
CAST_ROWS = 256
CAST_ROWS_WIDE = 128
FFN_CHUNK_TILES = (11,)
assert sum(FFN_CHUNK_TILES) * MXU_TILE == FFN_HIDDEN
assert CONV_WIDTH - 1 <= CONV_HALO and CONV_HALO % SUBLANES == 0


def _params(n_axes):
    return pltpu.CompilerParams(
        dimension_semantics=("arbitrary",) * n_axes,
        vmem_limit_bytes=VMEM_LIMIT_BYTES)


def _resident(shape):
    return pl.BlockSpec(shape, lambda *_: (0,) * len(shape),
                        pipeline_mode=pl.Buffered(1))


def _tok_spec(tile):
    return pl.BlockSpec((None, tile, D_MODEL), lambda b, s: (b, s, 0))


def _rms(x, g):
    return x * lax.rsqrt(jnp.mean(x * x, axis=-1, keepdims=True) + EPS) * g


def _ln(x):
    mu = jnp.mean(x, axis=-1, keepdims=True)
    xc = x - mu
    return xc * lax.rsqrt(jnp.mean(xc * xc, axis=-1, keepdims=True) + EPS)


def _mm(a, b):
    return jnp.dot(a, b, preferred_element_type=F32)


def _load_bf16(src, dst_ref, stage_ref, sem_ref, scale=None):
    slots, chunk = stage_ref.shape[0], stage_ref.shape[1]
    n = dst_ref.shape[0] // chunk

    def copy(j):
        return pltpu.make_async_copy(src.at[pl.ds(j * chunk, chunk), :],
                                     stage_ref.at[j % slots], sem_ref.at[j % slots])

    for j in range(min(slots - 1, n)):
        copy(j).start()
    for j in range(n):
        if j + slots - 1 < n:
            copy(j + slots - 1).start()
        copy(j).wait()
        w = stage_ref[j % slots]
        dst_ref[j * chunk:(j + 1) * chunk, :] = (w if scale is None else w * scale).astype(BF16)


def _stage(chunk_rows, cols):
    return pltpu.VMEM((CAST_SLOTS, chunk_rows, cols), F32)


def _even_kernel(x_ref, g_ref, win_hbm, gw_ref, gb_ref, cw_ref, cb_ref, lng_ref,
                 lnb_ref, wout_hbm, o_ref, hext_ref, conv_ref, win_ref, wout_ref,
                 stage_in, stage_out, sem, *, layer):
    tm = EVEN_TILE
    nblk = tm // GMLP_BLOCK

    @pl.when((pl.program_id(0) == 0) & (pl.program_id(1) == 0))
    def _():
        _load_bf16(win_hbm.at[layer], win_ref, stage_in, sem)
        _load_bf16(wout_hbm.at[layer], wout_ref, stage_out, sem)

    x = x_ref[...]
    hn = _rms(x, g_ref[...]).astype(BF16)
    proj = _mm(hn, win_ref[...])

    u = jax.nn.gelu(proj[:, :A_WIDTH])
    v = _ln(jax.nn.gelu(proj[:, A_WIDTH:2 * A_WIDTH])).astype(BF16)
    ri = lax.broadcasted_iota(jnp.int32, (GMLP_BLOCK, GMLP_BLOCK), 0) >> CHUNK_SHIFT
    ci = lax.broadcasted_iota(jnp.int32, (GMLP_BLOCK, GMLP_BLOCK), 1) >> CHUNK_SHIFT
    keep = ci <= ri
    per_group = []
    for g in range(A_GROUPS):
        gs = slice(g * A_GROUP_DIM, (g + 1) * A_GROUP_DIM)
        rhs = jnp.concatenate(
            [v[nb * GMLP_BLOCK:(nb + 1) * GMLP_BLOCK, gs] for nb in range(nblk)], axis=1)
        wm = jnp.where(keep, gw_ref[g], 0.0).astype(BF16)
        per_group.append(_mm(wm, rhs))
    sg = jnp.concatenate(
        [jnp.concatenate([per_group[g][:, nb * A_GROUP_DIM:(nb + 1) * A_GROUP_DIM]
                          for g in range(A_GROUPS)], axis=1) for nb in range(nblk)], axis=0)
    bias = jnp.concatenate([gb_ref[...]] * nblk, axis=0)
    out_a = u * (sg + bias)

    h = proj[:, 2 * A_WIDTH:2 * A_WIDTH + B_WIDTH] * jax.nn.sigmoid(proj[:, 2 * A_WIDTH + B_WIDTH:])

    @pl.when(pl.program_id(1) == 0)
    def _():
        hext_ref[0:CONV_HALO, :] = jnp.zeros((CONV_HALO, B_WIDTH), F32)

    hext_ref[CONV_HALO:CONV_HALO + tm, :] = h
    for cb in range(B_WIDTH // LANES):
        cs = slice(cb * LANES, (cb + 1) * LANES)
        for rc in range(tm // CONV_ROWS):
            r0 = rc * CONV_ROWS
            win = hext_ref[r0:r0 + CONV_ROWS + CONV_HALO, cs]
            acc = jnp.zeros((CONV_ROWS, LANES), F32)
            for r in range(SUBLANES):
                shifted = win if r == 0 else pltpu.roll(win, r, axis=0)
                for q in range(CONV_HALO // SUBLANES):
                    lag = SUBLANES * q + r
                    if lag < CONV_WIDTH:
                        k = CONV_WIDTH - 1 - lag
                        lo = CONV_HALO - SUBLANES * q
                        acc = acc + cw_ref[k:k + 1, cs] * shifted[lo:lo + CONV_ROWS, :]
            conv_ref[r0:r0 + CONV_ROWS, cs] = acc
    hext_ref[0:CONV_HALO, :] = hext_ref[tm:tm + CONV_HALO, :]
    c = _ln(conv_ref[...] + cb_ref[...]) * lng_ref[...] + lnb_ref[...]
    out_b = jax.nn.silu(c)

    mix = (_mm(out_a.astype(BF16), wout_ref[0:A_WIDTH, :])
           + _mm(out_b.astype(BF16), wout_ref[A_WIDTH:MIX_WIDTH, :]))
    o_ref[...] = x + mix


def _even_layer(x, norm, w_in_all, gmlp_w, gmlp_b, conv_w, conv_b, ln_g, ln_b, w_out_all, layer):
    gb_full = jnp.repeat(gmlp_b.T, A_GROUP_DIM, axis=1)
    row = lambda a: a.reshape(1, -1)
    hbm = pl.BlockSpec(memory_space=pl.ANY)
    return pl.pallas_call(
        functools.partial(_even_kernel, layer=layer),
        grid=(BATCH, SEQ // EVEN_TILE),
        in_specs=[
            _tok_spec(EVEN_TILE),
            _resident((1, D_MODEL)),
            hbm,
            _resident((A_GROUPS, GMLP_BLOCK, GMLP_BLOCK)),
            _resident((GMLP_BLOCK, A_WIDTH)),
            _resident((CONV_WIDTH, B_WIDTH)),
            _resident((1, B_WIDTH)),
            _resident((1, B_WIDTH)),
            _resident((1, B_WIDTH)),
            hbm,
        ],
        out_specs=_tok_spec(EVEN_TILE),
        out_shape=jax.ShapeDtypeStruct((BATCH, SEQ, D_MODEL), F32),
        scratch_shapes=[pltpu.VMEM((CONV_HALO + EVEN_TILE, B_WIDTH), F32),
                        pltpu.VMEM((EVEN_TILE, B_WIDTH), F32),
                        pltpu.VMEM((D_MODEL, IN_WIDTH), BF16),
                        pltpu.VMEM((MIX_WIDTH, D_MODEL), BF16),
                        _stage(CAST_ROWS_WIDE, IN_WIDTH),
                        _stage(CAST_ROWS, D_MODEL),
                        pltpu.SemaphoreType.DMA((CAST_SLOTS,))],
        compiler_params=_params(2),
        name="even_mixer",
    )(x, row(norm), w_in_all, gmlp_w, gb_full, conv_w, row(conv_b), row(ln_g),
      row(ln_b), w_out_all)


def _odd_kernel(x_ref, g_ref, win_hbm, bblk_ref, ar_ref, ai_ref, cblk_ref, d_ref, wout_hbm,
                o_ref, xs_ref, st_ref, win_ref, wout_ref, stage_in, stage_out, sem, *, layer):
    steps = SCAN_STEPS
    rows = steps * BATCH

    @pl.when(pl.program_id(0) == 0)
    def _():
        st_ref[...] = jnp.zeros(st_ref.shape, F32)
        _load_bf16(win_hbm.at[layer], win_ref, stage_in, sem)
        _load_bf16(wout_hbm.at[layer], wout_ref, stage_out, sem)

    x = x_ref[...].reshape(rows, D_MODEL)
    u_bt = _mm(_rms(x, g_ref[...]).astype(BF16), win_ref[...])
    u = jnp.swapaxes(u_bt.reshape(BATCH, steps, C_WIDTH), 0, 1).reshape(rows, C_WIDTH)
    ub = u.astype(BF16)
    for m in range(S5_SPLIT):
        bu = _mm(ub[:, m * S5_CH:(m + 1) * S5_CH], bblk_ref[m])
        xs_ref[:, m * S5_ST:(m + 1) * S5_ST] = bu[:, :S5_ST]
        xs_ref[:, S_LANES + m * S5_ST:S_LANES + (m + 1) * S5_ST] = bu[:, S5_ST:]

    for lb in range(S_LANES // SCAN_LANES):
        re = slice(lb * SCAN_LANES, (lb + 1) * SCAN_LANES)
        im = slice(S_LANES + lb * SCAN_LANES, S_LANES + (lb + 1) * SCAN_LANES)
        ar = jnp.broadcast_to(ar_ref[:, re], (BATCH, SCAN_LANES))
        ai = jnp.broadcast_to(ai_ref[:, re], (BATCH, SCAN_LANES))

        def step(t, carry):
            sr, si = carry
            r = pl.ds(pl.multiple_of(t * BATCH, BATCH), BATCH)
            nr = ar * sr - ai * si + xs_ref[r, re]
            ni = ar * si + ai * sr + xs_ref[r, im]
            xs_ref[r, re] = nr
            xs_ref[r, im] = ni
            return nr, ni

        sr, si = lax.fori_loop(0, steps, step, (st_ref[:, re], st_ref[:, im]),
                               unroll=SCAN_UNROLL)
        st_ref[:, re] = sr
        st_ref[:, im] = si

    ys = []
    for m in range(S5_SPLIT):
        xr = xs_ref[:, m * S5_ST:(m + 1) * S5_ST].astype(BF16)
        xi = xs_ref[:, S_LANES + m * S5_ST:S_LANES + (m + 1) * S5_ST].astype(BF16)
        ys.append(_mm(xr, cblk_ref[m, 0:S5_ST, :]) + _mm(xi, cblk_ref[m, S5_ST:2 * S5_ST, :]))
    y = jnp.concatenate(ys, axis=1) + d_ref[...] * u
    y_bt = jnp.swapaxes(y.reshape(steps, BATCH, C_WIDTH), 0, 1).reshape(rows, C_WIDTH)
    o = _mm(jax.nn.gelu(y_bt).astype(BF16), wout_ref[...])
    out = x + o[:, :D_MODEL] * jax.nn.sigmoid(o[:, D_MODEL:])
    o_ref[...] = out.reshape(BATCH, steps, D_MODEL)


def _s5_discretize(lam_re, lam_im, log_dt, b_re, b_im, c_re, c_im):
    dt = jnp.exp(log_dt)[:, None]
    mag = jnp.exp(lam_re * dt)
    ar = mag * jnp.cos(lam_im * dt)
    ai = mag * jnp.sin(lam_im * dt)
    den = lam_re * lam_re + lam_im * lam_im
    qr = ((ar - 1.0) * lam_re + ai * lam_im) / den
    qi = (ai * lam_re - (ar - 1.0) * lam_im) / den
    bbr = qr[..., None] * b_re - qi[..., None] * b_im
    bbi = qr[..., None] * b_im + qi[..., None] * b_re
    gs = C_GROUPS // S5_SPLIT
    eye = jnp.eye(gs, dtype=F32)
    split = lambda a: a.reshape((S5_SPLIT, gs) + a.shape[1:])
    blk_b = lambda a: jnp.einsum('mgpc,gh->mgchp', split(a), eye).reshape(S5_SPLIT, S5_CH, S5_ST)
    blk_c = lambda a: jnp.einsum('mgcp,gh->mgphc', split(a), eye).reshape(S5_SPLIT, S5_ST, S5_CH)
    bblk = jnp.concatenate([blk_b(bbr), blk_b(bbi)], axis=2)
    cblk = jnp.concatenate([blk_c(c_re), -blk_c(c_im)], axis=1)
    return (bblk.astype(BF16), ar.reshape(1, S_LANES), ai.reshape(1, S_LANES), cblk.astype(BF16))


def _odd_layer(x, norm, w_in_all, ssm, d_skip, w_out_all, layer):
    bblk, ar, ai, cblk = ssm
    spec = pl.BlockSpec((BATCH, SCAN_STEPS, D_MODEL), lambda i: (0, i, 0))
    hbm = pl.BlockSpec(memory_space=pl.ANY)
    return pl.pallas_call(
        functools.partial(_odd_kernel, layer=layer),
        grid=(SEQ // SCAN_STEPS,),
        in_specs=[
            spec,
            _resident((1, D_MODEL)),
            hbm,
            _resident((S5_SPLIT, S5_CH, 2 * S5_ST)),
            _resident((1, S_LANES)),
            _resident((1, S_LANES)),
            _resident((S5_SPLIT, 2 * S5_ST, S5_CH)),
            _resident((1, C_WIDTH)),
            hbm,
        ],
        out_specs=spec,
        out_shape=jax.ShapeDtypeStruct((BATCH, SEQ, D_MODEL), F32),
        scratch_shapes=[pltpu.VMEM((SCAN_STEPS * BATCH, 2 * S_LANES), F32),
                        pltpu.VMEM((BATCH, 2 * S_LANES), F32),
                        pltpu.VMEM((D_MODEL, C_WIDTH), BF16),
                        pltpu.VMEM((C_WIDTH, 2 * D_MODEL), BF16),
                        _stage(CAST_ROWS, C_WIDTH),
                        _stage(CAST_ROWS_WIDE, 2 * D_MODEL),
                        pltpu.SemaphoreType.DMA((CAST_SLOTS,))],
        compiler_params=_params(1),
        name="odd_mixer",
    )(x, norm.reshape(1, -1), w_in_all, bblk, ar, ai, cblk, d_skip.reshape(1, -1), w_out_all)


def _ca_kernel(x_ref, m_ref, g_ref, gm_ref, wq_hbm, wk_hbm, wv_hbm, wo_hbm, o_ref,
               wq_ref, wk_ref, wv_ref, wo_ref, kt_ref, v_ref, stage, sem, *, layer):
    @pl.when((pl.program_id(0) == 0) & (pl.program_id(1) == 0))
    def _():
        _load_bf16(wq_hbm.at[layer], wq_ref, stage, sem, scale=CA_HEAD_DIM ** -0.5)
        _load_bf16(wk_hbm.at[layer], wk_ref, stage, sem)
        _load_bf16(wv_hbm.at[layer], wv_ref, stage, sem)
        _load_bf16(wo_hbm.at[layer], wo_ref, stage, sem)

    @pl.when(pl.program_id(1) == 0)
    def _():
        mn = _rms(m_ref[...], gm_ref[...]).astype(BF16)
        kt_ref[...] = _mm(mn, wk_ref[...]).T.astype(BF16)
        v_ref[...] = _mm(mn, wv_ref[...]).astype(BF16)

    rows = CA_TILE // CA_SPLIT
    for blk in range(CA_SPLIT):
        rs = slice(blk * rows, (blk + 1) * rows)
        x = x_ref[rs, :]
        q = _mm(_rms(x, g_ref[...]).astype(BF16), wq_ref[...]).astype(BF16)
        heads = []
        for h in range(CA_HEADS):
            hs = slice(h * CA_HEAD_DIM, (h + 1) * CA_HEAD_DIM)
            sc = _mm(q[:, hs], kt_ref[hs, :])
            e = jnp.exp(sc - jnp.max(sc, axis=-1, keepdims=True))
            heads.append(_mm(e.astype(BF16), v_ref[:, hs]) / jnp.sum(e, axis=-1, keepdims=True))
        o = jnp.concatenate(heads, axis=1).astype(BF16)
        o_ref[rs, :] = x + _mm(o, wo_ref[...])


def _cross_attention(x, mem, norm, mem_norm, wq_all, wk_all, wv_all, wo_all, layer):
    mem_spec = pl.BlockSpec((None, MEM_LEN, D_MODEL), lambda b, s: (b, 0, 0))
    hbm = pl.BlockSpec(memory_space=pl.ANY)
    return pl.pallas_call(
        functools.partial(_ca_kernel, layer=layer),
        grid=(BATCH, SEQ // CA_TILE),
        in_specs=[_tok_spec(CA_TILE), mem_spec, _resident((1, D_MODEL)), _resident((1, D_MODEL)),
                  hbm, hbm, hbm, hbm],
        out_specs=_tok_spec(CA_TILE),
        out_shape=jax.ShapeDtypeStruct((BATCH, SEQ, D_MODEL), F32),
        scratch_shapes=([pltpu.VMEM((D_MODEL, D_MODEL), BF16)] * 4
                        + [pltpu.VMEM((D_MODEL, MEM_LEN), BF16),
                           pltpu.VMEM((MEM_LEN, D_MODEL), BF16),
                           _stage(CAST_ROWS, D_MODEL),
                           pltpu.SemaphoreType.DMA((CAST_SLOTS,))]),
        compiler_params=_params(2),
        name="cross_attention",
    )(x, mem, norm.reshape(1, -1), mem_norm.reshape(1, -1), wq_all, wk_all, wv_all, wo_all)


def _ffn_kernel(x_ref, g_ref, wg_hbm, wu_hbm, wd_hbm, fg_ref, o_ref,
                wg_ref, wu_ref, wd_ref, stage_in, stage_out, sem, *, layer, final):
    @pl.when((pl.program_id(0) == 0) & (pl.program_id(1) == 0))
    def _():
        _load_bf16(wg_hbm.at[layer], wg_ref, stage_in, sem)
        _load_bf16(wu_hbm.at[layer], wu_ref, stage_in, sem)
        _load_bf16(wd_hbm.at[layer], wd_ref, stage_out, sem)

    rows = FFN_TILE // FFN_SPLIT
    for blk in range(FFN_SPLIT):
        rs = slice(blk * rows, (blk + 1) * rows)
        x = x_ref[rs, :]
        xn = _rms(x, g_ref[...]).astype(BF16)
        y = x
        lo = 0
        for tiles in FFN_CHUNK_TILES:
            cs = slice(lo, lo + tiles * MXU_TILE)
            lo += tiles * MXU_TILE
            h = jax.nn.silu(_mm(xn, wg_ref[:, cs])) * _mm(xn, wu_ref[:, cs])
            y = y + _mm(h.astype(BF16), wd_ref[cs, :])
        if final:
            y = _rms(y, fg_ref[...])
        o_ref[rs, :] = y


def _ffn(x, norm, wg_all, wu_all, wd_all, final_norm, layer, final):
    hbm = pl.BlockSpec(memory_space=pl.ANY)
    return pl.pallas_call(
        functools.partial(_ffn_kernel, layer=layer, final=final),
        grid=(BATCH, SEQ // FFN_TILE),
        in_specs=[_tok_spec(FFN_TILE), _resident((1, D_MODEL)), hbm, hbm, hbm,
                  _resident((1, D_MODEL))],
        out_specs=_tok_spec(FFN_TILE),
        out_shape=jax.ShapeDtypeStruct((BATCH, SEQ, D_MODEL), F32),
        scratch_shapes=[pltpu.VMEM((D_MODEL, FFN_HIDDEN), BF16),
                        pltpu.VMEM((D_MODEL, FFN_HIDDEN), BF16),
                        pltpu.VMEM((FFN_HIDDEN, D_MODEL), BF16),
                        _stage(CAST_ROWS_WIDE, FFN_HIDDEN),
                        _stage(CAST_ROWS, D_MODEL),
                        pltpu.SemaphoreType.DMA((CAST_SLOTS,))],
        compiler_params=_params(2),
        name="ffn_final" if final else "ffn",
    )(x, norm.reshape(1, -1), wg_all, wu_all, wd_all, final_norm.reshape(1, -1))


def kernel(x, mem, e_norm, e_w_in, e_gmlp_w, e_gmlp_b, e_conv_w, e_conv_b, e_conv_ln_g, e_conv_ln_b, e_w_out, o_norm, o_w_in, o_lam_re, o_lam_im, o_log_dt, o_b_re, o_b_im, o_c_re, o_c_im, o_d, o_w_out, ca_norm, ca_mem_norm, ca_wq, ca_wk, ca_wv, ca_wo, ffn_norm, ffn_w_gate, ffn_w_up, ffn_w_down, final_norm):
    for i in range(DEPTH):
        j = i // 2
        if i % 2 == 0:
            x = _even_layer(x, e_norm[j], e_w_in, e_gmlp_w[j], e_gmlp_b[j], e_conv_w[j],
                            e_conv_b[j], e_conv_ln_g[j], e_conv_ln_b[j], e_w_out, layer=j)
        else:
            ssm = _s5_discretize(o_lam_re[j], o_lam_im[j], o_log_dt[j], o_b_re[j], o_b_im[j],
                                 o_c_re[j], o_c_im[j])
            x = _odd_layer(x, o_norm[j], o_w_in, ssm, o_d[j], o_w_out, layer=j)
        x = _cross_attention(x, mem, ca_norm[i], ca_mem_norm[i], ca_wq, ca_wk, ca_wv, ca_wo,
                             layer=i)
        x = _ffn(x, ffn_norm[i], ffn_w_gate, ffn_w_up, ffn_w_down, final_norm,
                 layer=i, final=(i == DEPTH - 1))
    return x
```

```python
import functools

import jax
import jax.numpy as jnp
from jax import lax
from jax.experimental import pallas as pl
from jax.experimental.pallas import tpu as pltpu

F32 = jnp.float32
BF16 = jnp.bfloat16

D_MODEL = 1024
BATCH = 8
SEQ = 4096
DEPTH = 2
CHUNK = 64
CHUNK_SHIFT = CHUNK.bit_length() - 1
MEM_LEN = 256
EPS = 1e-6
A_WIDTH = 512
A_GROUPS = 4
A_GROUP_DIM = A_WIDTH // A_GROUPS
GMLP_BLOCK = 128
B_WIDTH = 512
CONV_WIDTH = 31
MIX_WIDTH = A_WIDTH + B_WIDTH
IN_WIDTH = 2 * A_WIDTH + 2 * B_WIDTH
C_WIDTH = 512
C_GROUP_CH = 16
C_GROUPS = C_WIDTH // C_GROUP_CH
C_STATE = 64
S_LANES = C_GROUPS * C_STATE
S5_SPLIT = 2
S5_CH = C_WIDTH // S5_SPLIT
S5_ST = S_LANES // S5_SPLIT
CA_HEADS = 4
CA_HEAD_DIM = D_MODEL // CA_HEADS
FFN_HIDDEN = -(-8 * D_MODEL // (3 * 256)) * 256

VMEM_LIMIT_BYTES = 56 * 1024 * 1024
SUBLANES = 8
LANES = 128
MXU_TILE = 256

EVEN_TILE = 1024
CA_TILE = 2048
CA_SPLIT = 4
FFN_TILE = 1024
FFN_SPLIT = 2
CONV_HALO = 32
CONV_ROWS = 128
SCAN_STEPS = 128
SCAN_LANES = 1024
SCAN_UNROLL = 4
CAST_SLOTS = 3
CAST_ROWS = 256
CAST_ROWS_WIDE = 128
assert FFN_HIDDEN % MXU_TILE == 0 and D_MODEL % MXU_TILE == 0
assert CONV_WIDTH - 1 <= CONV_HALO and CONV_HALO % SUBLANES == 0


def _params(n_axes):
    return pltpu.CompilerParams(
        dimension_semantics=("arbitrary",) * n_axes,
        vmem_limit_bytes=VMEM_LIMIT_BYTES)


def _resident(shape):
    return pl.BlockSpec(shape, lambda *_: (0,) * len(shape),
                        pipeline_mode=pl.Buffered(1))


def _tok_spec(tile):
    return pl.BlockSpec((None, tile, D_MODEL), lambda b, s: (b, s, 0))


def _rms(x, g):
    return x * lax.rsqrt(jnp.mean(x * x, axis=-1, keepdims=True) + EPS) * g


def _ln(x):
    mu = jnp.mean(x, axis=-1, keepdims=True)
    xc = x - mu
    return xc * lax.rsqrt(jnp.mean(xc * xc, axis=-1, keepdims=True) + EPS)


def _mm(a, b):
    return jnp.dot(a, b, preferred_element_type=F32)


def _load_bf16(src, dst_ref, stage_ref, sem_ref, scale=None):
    slots, chunk = stage_ref.shape[0], stage_ref.shape[1]
    n = dst_ref.shape[0] // chunk

    def copy(j):
        return pltpu.make_async_copy(src.at[pl.ds(j * chunk, chunk), :],
                                     stage_ref.at[j % slots], sem_ref.at[j % slots])

    for j in range(min(slots - 1, n)):
        copy(j).start()
    for j in range(n):
        if j + slots - 1 < n:
            copy(j + slots - 1).start()
        copy(j).wait()
        w = stage_ref[j % slots]
        dst_ref[j * chunk:(j + 1) * chunk, :] = (w if scale is None else w * scale).astype(BF16)


def _stage(chunk_rows, cols):
    return pltpu.VMEM((CAST_SLOTS, chunk_rows, cols), F32)


def _even_kernel(x_ref, g_ref, win_hbm, gw_ref, gb_ref, cw_ref, cb_ref, lng_ref,
                 lnb_ref, wout_hbm, o_ref, hext_ref, conv_ref, win_ref, wout_ref,
                 stage_in, stage_out, sem, *, layer):
    tm = EVEN_TILE
    nblk = tm // GMLP_BLOCK

    @pl.when((pl.program_id(0) == 0) & (pl.program_id(1) == 0))
    def _():
        _load_bf16(win_hbm.at[layer], win_ref, stage_in, sem)
        _load_bf16(wout_hbm.at[layer], wout_ref, stage_out, sem)

    x = x_ref[...]
    hn = _rms(x, g_ref[...]).astype(BF16)
    proj = _mm(hn, win_ref[...])

    u = jax.nn.gelu(proj[:, :A_WIDTH])
    v = _ln(jax.nn.gelu(proj[:, A_WIDTH:2 * A_WIDTH])).astype(BF16)
    ri = lax.broadcasted_iota(jnp.int32, (GMLP_BLOCK, GMLP_BLOCK), 0) >> CHUNK_SHIFT
    ci = lax.broadcasted_iota(jnp.int32, (GMLP_BLOCK, GMLP_BLOCK), 1) >> CHUNK_SHIFT
    keep = ci <= ri
    per_group = []
    for g in range(A_GROUPS):
        gs = slice(g * A_GROUP_DIM, (g + 1) * A_GROUP_DIM)
        rhs = jnp.concatenate(
            [v[nb * GMLP_BLOCK:(nb + 1) * GMLP_BLOCK, gs] for nb in range(nblk)], axis=1)
        wm = jnp.where(keep, gw_ref[g], 0.0).astype(BF16)
        per_group.append(_mm(wm, rhs))
    sg = jnp.concatenate(
        [jnp.concatenate([per_group[g][:, nb * A_GROUP_DIM:(nb + 1) * A_GROUP_DIM]
                          for g in range(A_GROUPS)], axis=1) for nb in range(nblk)], axis=0)
    bias = jnp.concatenate([gb_ref[...]] * nblk, axis=0)
    out_a = u * (sg + bias)

    h = proj[:, 2 * A_WIDTH:2 * A_WIDTH + B_WIDTH] * jax.nn.sigmoid(proj[:, 2 * A_WIDTH + B_WIDTH:])

    @pl.when(pl.program_id(1) == 0)
    def _():
        hext_ref[0:CONV_HALO, :] = jnp.zeros((CONV_HALO, B_WIDTH), F32)

    hext_ref[CONV_HALO:CONV_HALO + tm, :] = h
    for cb in range(B_WIDTH // LANES):
        cs = slice(cb * LANES, (cb + 1) * LANES)
        for rc in range(tm // CONV_ROWS):
            r0 = rc * CONV_ROWS
            win = hext_ref[r0:r0 + CONV_ROWS + CONV_HALO, cs]
            acc = jnp.zeros((CONV_ROWS, LANES), F32)
            for r in range(SUBLANES):
                shifted = win if r == 0 else pltpu.roll(win, r, axis=0)
                for q in range(CONV_HALO // SUBLANES):
                    lag = SUBLANES * q + r
                    if lag < CONV_WIDTH:
                        k = CONV_WIDTH - 1 - lag
                        lo = CONV_HALO - SUBLANES * q
                        acc = acc + cw_ref[k:k + 1, cs] * shifted[lo:lo + CONV_ROWS, :]
            conv_ref[r0:r0 + CONV_ROWS, cs] = acc
    hext_ref[0:CONV_HALO, :] = hext_ref[tm:tm + CONV_HALO, :]
    c = _ln(conv_ref[...] + cb_ref[...]) * lng_ref[...] + lnb_ref[...]
    out_b = jax.nn.silu(c)

    mix = (_mm(out_a.astype(BF16), wout_ref[0:A_WIDTH, :])
           + _mm(out_b.astype(BF16), wout_ref[A_WIDTH:MIX_WIDTH, :]))
    o_ref[...] = x + mix


def _even_layer(x, norm, w_in_all, gmlp_w, gmlp_b, conv_w, conv_b, ln_g, ln_b, w_out_all, layer):
    gb_full = jnp.repeat(gmlp_b.T, A_GROUP_DIM, axis=1)
    row = lambda a: a.reshape(1, -1)
    hbm = pl.BlockSpec(memory_space=pl.ANY)
    return pl.pallas_call(
        functools.partial(_even_kernel, layer=layer),
        grid=(BATCH, SEQ // EVEN_TILE),
        in_specs=[
            _tok_spec(EVEN_TILE),
            _resident((1, D_MODEL)),
            hbm,
            _resident((A_GROUPS, GMLP_BLOCK, GMLP_BLOCK)),
            _resident((GMLP_BLOCK, A_WIDTH)),
            _resident((CONV_WIDTH, B_WIDTH)),
            _resident((1, B_WIDTH)),
            _resident((1, B_WIDTH)),
            _resident((1, B_WIDTH)),
            hbm,
        ],
        out_specs=_tok_spec(EVEN_TILE),
        out_shape=jax.ShapeDtypeStruct((BATCH, SEQ, D_MODEL), F32),
        scratch_shapes=[pltpu.VMEM((CONV_HALO + EVEN_TILE, B_WIDTH), F32),
                        pltpu.VMEM((EVEN_TILE, B_WIDTH), F32),
                        pltpu.VMEM((D_MODEL, IN_WIDTH), BF16),
                        pltpu.VMEM((MIX_WIDTH, D_MODEL), BF16),
                        _stage(CAST_ROWS_WIDE, IN_WIDTH),
                        _stage(CAST_ROWS, D_MODEL),
                        pltpu.SemaphoreType.DMA((CAST_SLOTS,))],
        compiler_params=_params(2),
        name="even_mixer",
    )(x, row(norm), w_in_all, gmlp_w, gb_full, conv_w, row(conv_b), row(ln_g),
      row(ln_b), w_out_all)


def _odd_kernel(x_ref, g_ref, win_hbm, bblk_ref, ar_ref, ai_ref, cblk_ref, d_ref, wout_hbm,
                o_ref, xs_ref, st_ref, win_ref, wout_ref, stage_in, stage_out, sem, *, layer):
    steps = SCAN_STEPS
    rows = steps * BATCH

    @pl.when(pl.program_id(0) == 0)
    def _():
        st_ref[...] = jnp.zeros(st_ref.shape, F32)
        _load_bf16(win_hbm.at[layer], win_ref, stage_in, sem)
        _load_bf16(wout_hbm.at[layer], wout_ref, stage_out, sem)

    x = x_ref[...].reshape(rows, D_MODEL)
    u_bt = _mm(_rms(x, g_ref[...]).astype(BF16), win_ref[...])
    u = jnp.swapaxes(u_bt.reshape(BATCH, steps, C_WIDTH), 0, 1).reshape(rows, C_WIDTH)
    ub = u.astype(BF16)
    for m in range(S5_SPLIT):
        bu = _mm(ub[:, m * S5_CH:(m + 1) * S5_CH], bblk_ref[m])
        xs_ref[:, m * S5_ST:(m + 1) * S5_ST] = bu[:, :S5_ST]
        xs_ref[:, S_LANES + m * S5_ST:S_LANES + (m + 1) * S5_ST] = bu[:, S5_ST:]

    for lb in range(S_LANES // SCAN_LANES):
        re = slice(lb * SCAN_LANES, (lb + 1) * SCAN_LANES)
        im = slice(S_LANES + lb * SCAN_LANES, S_LANES + (lb + 1) * SCAN_LANES)
        ar = jnp.broadcast_to(ar_ref[:, re], (BATCH, SCAN_LANES))
        ai = jnp.broadcast_to(ai_ref[:, re], (BATCH, SCAN_LANES))

        def step(t, carry):
            sr, si = carry
            r = pl.ds(pl.multiple_of(t * BATCH, BATCH), BATCH)
            nr = ar * sr - ai * si + xs_ref[r, re]
            ni = ar * si + ai * sr + xs_ref[r, im]
            xs_ref[r, re] = nr
            xs_ref[r, im] = ni
            return nr, ni

        sr, si = lax.fori_loop(0, steps, step, (st_ref[:, re], st_ref[:, im]),
                               unroll=SCAN_UNROLL)
        st_ref[:, re] = sr
        st_ref[:, im] = si

    ys = []
    for m in range(S5_SPLIT):
        xr = xs_ref[:, m * S5_ST:(m + 1) * S5_ST].astype(BF16)
        xi = xs_ref[:, S_LANES + m * S5_ST:S_LANES + (m + 1) * S5_ST].astype(BF16)
        ys.append(_mm(xr, cblk_ref[m, 0:S5_ST, :]) + _mm(xi, cblk_ref[m, S5_ST:2 * S5_ST, :]))
    y = jnp.concatenate(ys, axis=1) + d_ref[...] * u
    y_bt = jnp.swapaxes(y.reshape(steps, BATCH, C_WIDTH), 0, 1).reshape(rows, C_WIDTH)
    gy = jax.nn.gelu(y_bt).astype(BF16)
    outs = []
    for c in range(0, D_MODEL, MXU_TILE):
        val = _mm(gy, wout_ref[:, c:c + MXU_TILE])
        gate = _mm(gy, wout_ref[:, D_MODEL + c:D_MODEL + c + MXU_TILE])
        outs.append(x[:, c:c + MXU_TILE] + val * jax.nn.sigmoid(gate))
    o_ref[...] = jnp.concatenate(outs, axis=1).reshape(BATCH, steps, D_MODEL)


def _s5_discretize(lam_re, lam_im, log_dt, b_re, b_im, c_re, c_im):
    dt = jnp.exp(log_dt)[:, None]
    mag = jnp.exp(lam_re * dt)
    ar = mag * jnp.cos(lam_im * dt)
    ai = mag * jnp.sin(lam_im * dt)
    den = lam_re * lam_re + lam_im * lam_im
    qr = ((ar - 1.0) * lam_re + ai * lam_im) / den
    qi = (ai * lam_re - (ar - 1.0) * lam_im) / den
    bbr = qr[..., None] * b_re - qi[..., None] * b_im
    bbi = qr[..., None] * b_im + qi[..., None] * b_re
    gs = C_GROUPS // S5_SPLIT
    eye = jnp.eye(gs, dtype=F32)
    split = lambda a: a.reshape((S5_SPLIT, gs) + a.shape[1:])
    blk_b = lambda a: jnp.einsum('mgpc,gh->mgchp', split(a), eye).reshape(S5_SPLIT, S5_CH, S5_ST)
    blk_c = lambda a: jnp.einsum('mgcp,gh->mgphc', split(a), eye).reshape(S5_SPLIT, S5_ST, S5_CH)
    bblk = jnp.concatenate([blk_b(bbr), blk_b(bbi)], axis=2)
    cblk = jnp.concatenate([blk_c(c_re), -blk_c(c_im)], axis=1)
    return (bblk.astype(BF16), ar.reshape(1, S_LANES), ai.reshape(1, S_LANES), cblk.astype(BF16))


def _odd_layer(x, norm, w_in_all, ssm, d_skip, w_out_all, layer):
    bblk, ar, ai, cblk = ssm
    spec = pl.BlockSpec((BATCH, SCAN_STEPS, D_MODEL), lambda i: (0, i, 0))
    hbm = pl.BlockSpec(memory_space=pl.ANY)
    return pl.pallas_call(
        functools.partial(_odd_kernel, layer=layer),
        grid=(SEQ // SCAN_STEPS,),
        in_specs=[
            spec,
            _resident((1, D_MODEL)),
            hbm,
            _resident((S5_SPLIT, S5_CH, 2 * S5_ST)),
            _resident((1, S_LANES)),
            _resident((1, S_LANES)),
            _resident((S5_SPLIT, 2 * S5_ST, S5_CH)),
            _resident((1, C_WIDTH)),
            hbm,
        ],
        out_specs=spec,
        out_shape=jax.ShapeDtypeStruct((BATCH, SEQ, D_MODEL), F32),
        scratch_shapes=[pltpu.VMEM((SCAN_STEPS * BATCH, 2 * S_LANES), F32),
                        pltpu.VMEM((BATCH, 2 * S_LANES), F32),
                        pltpu.VMEM((D_MODEL, C_WIDTH), BF16),
                        pltpu.VMEM((C_WIDTH, 2 * D_MODEL), BF16),
                        _stage(CAST_ROWS, C_WIDTH),
                        _stage(CAST_ROWS_WIDE, 2 * D_MODEL),
                        pltpu.SemaphoreType.DMA((CAST_SLOTS,))],
        compiler_params=_params(1),
        name="odd_mixer",
    )(x, norm.reshape(1, -1), w_in_all, bblk, ar, ai, cblk, d_skip.reshape(1, -1), w_out_all)


def _ca_kernel(x_ref, m_ref, g_ref, gm_ref, wq_hbm, wk_hbm, wv_hbm, wo_hbm, o_ref,
               wq_ref, wk_ref, wv_ref, wo_ref, kt_ref, v_ref, stage, sem, *, layer):
    @pl.when((pl.program_id(0) == 0) & (pl.program_id(1) == 0))
    def _():
        _load_bf16(wq_hbm.at[layer], wq_ref, stage, sem, scale=CA_HEAD_DIM ** -0.5)
        _load_bf16(wk_hbm.at[layer], wk_ref, stage, sem)
        _load_bf16(wv_hbm.at[layer], wv_ref, stage, sem)
        _load_bf16(wo_hbm.at[layer], wo_ref, stage, sem)

    @pl.when(pl.program_id(1) == 0)
    def _():
        mn = _rms(m_ref[...], gm_ref[...]).astype(BF16)
        kt_ref[...] = _mm(mn, wk_ref[...]).T.astype(BF16)
        v_ref[...] = _mm(mn, wv_ref[...]).astype(BF16)

    rows = CA_TILE // CA_SPLIT
    for blk in range(CA_SPLIT):
        rs = slice(blk * rows, (blk + 1) * rows)
        x = x_ref[rs, :]
        q = _mm(_rms(x, g_ref[...]).astype(BF16), wq_ref[...]).astype(BF16)
        heads = []
        for h in range(CA_HEADS):
            hs = slice(h * CA_HEAD_DIM, (h + 1) * CA_HEAD_DIM)
            sc = _mm(q[:, hs], kt_ref[hs, :])
            e = jnp.exp(sc - jnp.max(sc, axis=-1, keepdims=True))
            heads.append(_mm(e.astype(BF16), v_ref[:, hs]) / jnp.sum(e, axis=-1, keepdims=True))
        o = jnp.concatenate(heads, axis=1).astype(BF16)
        o_ref[rs, :] = x + _mm(o, wo_ref[...])


def _cross_attention(x, mem, norm, mem_norm, wq_all, wk_all, wv_all, wo_all, layer):
    mem_spec = pl.BlockSpec((None, MEM_LEN, D_MODEL), lambda b, s: (b, 0, 0))
    hbm = pl.BlockSpec(memory_space=pl.ANY)
    return pl.pallas_call(
        functools.partial(_ca_kernel, layer=layer),
        grid=(BATCH, SEQ // CA_TILE),
        in_specs=[_tok_spec(CA_TILE), mem_spec, _resident((1, D_MODEL)), _resident((1, D_MODEL)),
                  hbm, hbm, hbm, hbm],
        out_specs=_tok_spec(CA_TILE),
        out_shape=jax.ShapeDtypeStruct((BATCH, SEQ, D_MODEL), F32),
        scratch_shapes=([pltpu.VMEM((D_MODEL, D_MODEL), BF16)] * 4
                        + [pltpu.VMEM((D_MODEL, MEM_LEN), BF16),
                           pltpu.VMEM((MEM_LEN, D_MODEL), BF16),
                           _stage(CAST_ROWS, D_MODEL),
                           pltpu.SemaphoreType.DMA((CAST_SLOTS,))]),
        compiler_params=_params(2),
        name="cross_attention",
    )(x, mem, norm.reshape(1, -1), mem_norm.reshape(1, -1), wq_all, wk_all, wv_all, wo_all)


def _ffn_kernel(x_ref, g_ref, wg_hbm, wu_hbm, wd_hbm, fg_ref, o_ref,
                wg_ref, wu_ref, wd_ref, stage_in, stage_out, sem, *, layer, final):
    @pl.when((pl.program_id(0) == 0) & (pl.program_id(1) == 0))
    def _():
        _load_bf16(wg_hbm.at[layer], wg_ref, stage_in, sem)
        _load_bf16(wu_hbm.at[layer], wu_ref, stage_in, sem)
        _load_bf16(wd_hbm.at[layer], wd_ref, stage_out, sem)

    rows = FFN_TILE // FFN_SPLIT
    for blk in range(FFN_SPLIT):
        rs = slice(blk * rows, (blk + 1) * rows)
        x = x_ref[rs, :]
        xn = _rms(x, g_ref[...]).astype(BF16)
        hs = []
        for n in range(FFN_HIDDEN // MXU_TILE):
            cs = slice(n * MXU_TILE, (n + 1) * MXU_TILE)
            hs.append((jax.nn.silu(_mm(xn, wg_ref[:, cs])) * _mm(xn, wu_ref[:, cs])).astype(BF16))
        y = x + _mm(jnp.concatenate(hs, axis=1), wd_ref[...])
        if final:
            y = _rms(y, fg_ref[...])
        o_ref[rs, :] = y


def _ffn(x, norm, wg_all, wu_all, wd_all, final_norm, layer, final):
    hbm = pl.BlockSpec(memory_space=pl.ANY)
    return pl.pallas_call(
        functools.partial(_ffn_kernel, layer=layer, final=final),
        grid=(BATCH, SEQ // FFN_TILE),
        in_specs=[_tok_spec(FFN_TILE), _resident((1, D_MODEL)), hbm, hbm, hbm,
                  _resident((1, D_MODEL))],
        out_specs=_tok_spec(FFN_TILE),
        out_shape=jax.ShapeDtypeStruct((BATCH, SEQ, D_MODEL), F32),
        scratch_shapes=[pltpu.VMEM((D_MODEL, FFN_HIDDEN), BF16),
                        pltpu.VMEM((D_MODEL, FFN_HIDDEN), BF16),
                        pltpu.VMEM((FFN_HIDDEN, D_MODEL), BF16),
                        _stage(CAST_ROWS_WIDE, FFN_HIDDEN),
                        _stage(CAST_ROWS, D_MODEL),
                        pltpu.SemaphoreType.DMA((CAST_SLOTS,))],
        compiler_params=_params(2),
        name="ffn_final" if final else "ffn",
    )(x, norm.reshape(1, -1), wg_all, wu_all, wd_all, final_norm.reshape(1, -1))


def kernel(x, mem, e_norm, e_w_in, e_gmlp_w, e_gmlp_b, e_conv_w, e_conv_b, e_conv_ln_g, e_conv_ln_b, e_w_out, o_norm, o_w_in, o_lam_re, o_lam_im, o_log_dt, o_b_re, o_b_im, o_c_re, o_c_im, o_d, o_w_out, ca_norm, ca_mem_norm, ca_wq, ca_wk, ca_wv, ca_wo, ffn_norm, ffn_w_gate, ffn_w_up, ffn_w_down, final_norm):
    for i in range(DEPTH):
        j = i // 2
        if i % 2 == 0:
            x = _even_layer(x, e_norm[j], e_w_in, e_gmlp_w[j], e_gmlp_b[j], e_conv_w[j],
                            e_conv_b[j], e_conv_ln_g[j], e_conv_ln_b[j], e_w_out, layer=j)
        else:
            ssm = _s5_discretize(o_lam_re[j], o_lam_im[j], o_log_dt[j], o_b_re[j], o_b_im[j],
                                 o_c_re[j], o_c_im[j])
            x = _odd_layer(x, o_norm[j], o_w_in, ssm, o_d[j], o_w_out, layer=j)
        x = _cross_attention(x, mem, ca_norm[i], ca_mem_norm[i], ca_wq, ca_wk, ca_wv, ca_wo,
                             layer=i)
        x = _ffn(x, ffn_norm[i], ffn_w_gate, ffn_w_up, ffn_w_down, final_norm,
                 layer=i, final=(i == DEPTH - 1))
    return x
```

```python
import functools

import jax
import jax.numpy as jnp
from jax import lax
from jax.experimental import pallas as pl
from jax.experimental.pallas import tpu as pltpu

F32 = jnp.float32
BF16 = jnp.bfloat16

D_MODEL = 1024
BATCH = 8
SEQ = 4096
DEPTH = 2
CHUNK = 64
CHUNK_SHIFT = CHUNK.bit_length() - 1
MEM_LEN = 256
EPS = 1e-6
A_WIDTH = 512
A_GROUPS = 4
A_GROUP_DIM = A_WIDTH // A_GROUPS
GMLP_BLOCK = 128
B_WIDTH = 512
CONV_WIDTH = 31
MIX_WIDTH = A_WIDTH + B_WIDTH
IN_WIDTH = 2 * A_WIDTH + 2 * B_WIDTH
C_WIDTH = 512
C_GROUP_CH = 16
C_GROUPS = C_WIDTH // C_GROUP_CH
C_STATE = 64
S_LANES = C_GROUPS * C_STATE
S5_SPLIT = 2
S5_CH = C_WIDTH // S5_SPLIT
S5_ST = S_LANES // S5_SPLIT
CA_HEADS = 4
CA_HEAD_DIM = D_MODEL // CA_HEADS
FFN_HIDDEN = -(-8 * D_MODEL // (3 * 256)) * 256

VMEM_LIMIT_BYTES = 56 * 1024 * 1024
SUBLANES = 8
LANES = 128
MXU_TILE = 256

EVEN_TILE = 1024
CA_TILE = 2048
CA_SPLIT = 4
FFN_TILE = 1024
CONV_HALO = 32
CONV_ROWS = 128
CONV_UNROLL = 2
SCAN_STEPS = 128
SCAN_LANES = 1024
SCAN_UNROLL = 4
CAST_SLOTS = 3
CAST_ROWS = 256
CAST_ROWS_WIDE = 128
assert FFN_HIDDEN % MXU_TILE == 0 and D_MODEL % MXU_TILE == 0
assert CONV_WIDTH - 1 <= CONV_HALO and CONV_HALO % SUBLANES == 0


def _params(n_axes):
    return pltpu.CompilerParams(
        dimension_semantics=("arbitrary",) * n_axes,
        vmem_limit_bytes=VMEM_LIMIT_BYTES)


def _resident(shape):
    return pl.BlockSpec(shape, lambda *_: (0,) * len(shape),
                        pipeline_mode=pl.Buffered(1))


def _tok_spec(tile):
    return pl.BlockSpec((None, tile, D_MODEL), lambda b, s: (b, s, 0))


def _rms(x, g):
    return x * lax.rsqrt(jnp.mean(x * x, axis=-1, keepdims=True) + EPS) * g


def _ln(x):
    mu = jnp.mean(x, axis=-1, keepdims=True)
    xc = x - mu
    return xc * lax.rsqrt(jnp.mean(xc * xc, axis=-1, keepdims=True) + EPS)


def _mm(a, b):
    return jnp.dot(a, b, preferred_element_type=F32)


def _load_bf16(src, dst_ref, stage_ref, sem_ref, scale=None):
    slots, chunk = stage_ref.shape[0], stage_ref.shape[1]
    n = dst_ref.shape[0] // chunk

    def copy(j):
        return pltpu.make_async_copy(src.at[pl.ds(j * chunk, chunk), :],
                                     stage_ref.at[j % slots], sem_ref.at[j % slots])

    for j in range(min(slots - 1, n)):
        copy(j).start()
    for j in range(n):
        if j + slots - 1 < n:
            copy(j + slots - 1).start()
        copy(j).wait()
        w = stage_ref[j % slots]
        dst_ref[j * chunk:(j + 1) * chunk, :] = (w if scale is None else w * scale).astype(BF16)


def _stage(chunk_rows, cols):
    return pltpu.VMEM((CAST_SLOTS, chunk_rows, cols), F32)


def _even_kernel(x_ref, g_ref, win_hbm, gw_ref, gb_ref, cw_ref, cb_ref, lng_ref,
                 lnb_ref, wout_hbm, o_ref, hext_ref, conv_ref, win_ref, wout_ref,
                 stage_in, stage_out, sem, *, layer):
    tm = EVEN_TILE
    nblk = tm // GMLP_BLOCK

    @pl.when((pl.program_id(0) == 0) & (pl.program_id(1) == 0))
    def _():
        _load_bf16(win_hbm.at[layer], win_ref, stage_in, sem)
        _load_bf16(wout_hbm.at[layer], wout_ref, stage_out, sem)

    x = x_ref[...]
    hn = _rms(x, g_ref[...]).astype(BF16)
    proj = _mm(hn, win_ref[...])

    u = jax.nn.gelu(proj[:, :A_WIDTH])
    v = _ln(jax.nn.gelu(proj[:, A_WIDTH:2 * A_WIDTH])).astype(BF16)
    ri = lax.broadcasted_iota(jnp.int32, (GMLP_BLOCK, GMLP_BLOCK), 0) >> CHUNK_SHIFT
    ci = lax.broadcasted_iota(jnp.int32, (GMLP_BLOCK, GMLP_BLOCK), 1) >> CHUNK_SHIFT
    keep = ci <= ri
    per_group = []
    for g in range(A_GROUPS):
        gs = slice(g * A_GROUP_DIM, (g + 1) * A_GROUP_DIM)
        rhs = jnp.concatenate(
            [v[nb * GMLP_BLOCK:(nb + 1) * GMLP_BLOCK, gs] for nb in range(nblk)], axis=1)
        wm = jnp.where(keep, gw_ref[g], 0.0).astype(BF16)
        per_group.append(_mm(wm, rhs))
    sg = jnp.concatenate(
        [jnp.concatenate([per_group[g][:, nb * A_GROUP_DIM:(nb + 1) * A_GROUP_DIM]
                          for g in range(A_GROUPS)], axis=1) for nb in range(nblk)], axis=0)
    bias = jnp.concatenate([gb_ref[...]] * nblk, axis=0)
    out_a = u * (sg + bias)

    h = proj[:, 2 * A_WIDTH:2 * A_WIDTH + B_WIDTH] * jax.nn.sigmoid(proj[:, 2 * A_WIDTH + B_WIDTH:])

    lane_blocks = B_WIDTH // LANES
    row_blocks = tm // CONV_ROWS

    @pl.when(pl.program_id(1) == 0)
    def _():
        hext_ref[:, 0:CONV_HALO, :] = jnp.zeros((lane_blocks, CONV_HALO, LANES), F32)

    for cb in range(lane_blocks):
        hext_ref[cb, CONV_HALO:CONV_HALO + tm, :] = h[:, cb * LANES:(cb + 1) * LANES]

    def conv_blocks(it, carry):
        for j in range(CONV_UNROLL):
            blk = it * CONV_UNROLL + j
            cb = blk // row_blocks
            r0 = pl.multiple_of((blk % row_blocks) * CONV_ROWS, CONV_ROWS)
            win = hext_ref[cb, pl.ds(r0, CONV_ROWS + CONV_HALO), :]
            acc = jnp.zeros((CONV_ROWS, LANES), F32)
            for r in range(SUBLANES):
                shifted = win if r == 0 else pltpu.roll(win, r, axis=0)
                for q in range(CONV_HALO // SUBLANES):
                    lag = SUBLANES * q + r
                    if lag < CONV_WIDTH:
                        k = CONV_WIDTH - 1 - lag
                        lo = CONV_HALO - SUBLANES * q
                        acc = acc + cw_ref[cb, k:k + 1, :] * shifted[lo:lo + CONV_ROWS, :]
            conv_ref[cb, pl.ds(r0, CONV_ROWS), :] = acc
        return carry

    lax.fori_loop(0, lane_blocks * row_blocks // CONV_UNROLL, conv_blocks, 0)
    for cb in range(lane_blocks):
        hext_ref[cb, 0:CONV_HALO, :] = hext_ref[cb, tm:tm + CONV_HALO, :]
    conv = jnp.concatenate([conv_ref[cb] for cb in range(lane_blocks)], axis=1)
    c = _ln(conv + cb_ref[...]) * lng_ref[...] + lnb_ref[...]
    out_b = jax.nn.silu(c)

    mix = (_mm(out_a.astype(BF16), wout_ref[0:A_WIDTH, :])
           + _mm(out_b.astype(BF16), wout_ref[A_WIDTH:MIX_WIDTH, :]))
    o_ref[...] = x_ref[...] + mix


def _even_layer(x, norm, w_in_all, gmlp_w, gmlp_b, conv_w, conv_b, ln_g, ln_b, w_out_all, layer):
    gb_full = jnp.repeat(gmlp_b.T, A_GROUP_DIM, axis=1)
    row = lambda a: a.reshape(1, -1)
    hbm = pl.BlockSpec(memory_space=pl.ANY)
    lane_blocks = B_WIDTH // LANES
    conv_w_b = conv_w.reshape(CONV_WIDTH, lane_blocks, LANES).transpose(1, 0, 2)
    return pl.pallas_call(
        functools.partial(_even_kernel, layer=layer),
        grid=(BATCH, SEQ // EVEN_TILE),
        in_specs=[
            _tok_spec(EVEN_TILE),
            _resident((1, D_MODEL)),
            hbm,
            _resident((A_GROUPS, GMLP_BLOCK, GMLP_BLOCK)),
            _resident((GMLP_BLOCK, A_WIDTH)),
            _resident((lane_blocks, CONV_WIDTH, LANES)),
            _resident((1, B_WIDTH)),
            _resident((1, B_WIDTH)),
            _resident((1, B_WIDTH)),
            hbm,
        ],
        out_specs=_tok_spec(EVEN_TILE),
        out_shape=jax.ShapeDtypeStruct((BATCH, SEQ, D_MODEL), F32),
        scratch_shapes=[pltpu.VMEM((lane_blocks, CONV_HALO + EVEN_TILE, LANES), F32),
                        pltpu.VMEM((lane_blocks, EVEN_TILE, LANES), F32),
                        pltpu.VMEM((D_MODEL, IN_WIDTH), BF16),
                        pltpu.VMEM((MIX_WIDTH, D_MODEL), BF16),
                        _stage(CAST_ROWS_WIDE, IN_WIDTH),
                        _stage(CAST_ROWS, D_MODEL),
                        pltpu.SemaphoreType.DMA((CAST_SLOTS,))],
        compiler_params=_params(2),
        name="even_mixer",
    )(x, row(norm), w_in_all, gmlp_w, gb_full, conv_w_b, row(conv_b), row(ln_g),
      row(ln_b), w_out_all)


def _odd_kernel(x_ref, g_ref, win_hbm, bblk_ref, ar_ref, ai_ref, cblk_ref, d_ref, wout_hbm,
                o_ref, xs_ref, st_ref, win_ref, wout_ref, stage_in, stage_out, sem, *, layer):
    steps = SCAN_STEPS
    rows = steps * BATCH

    @pl.when(pl.program_id(0) == 0)
    def _():
        st_ref[...] = jnp.zeros(st_ref.shape, F32)
        _load_bf16(win_hbm.at[layer], win_ref, stage_in, sem)
        _load_bf16(wout_hbm.at[layer], wout_ref, stage_out, sem)

    x = x_ref[...].reshape(rows, D_MODEL)
    u_bt = _mm(_rms(x, g_ref[...]).astype(BF16), win_ref[...])
    u = jnp.swapaxes(u_bt.reshape(BATCH, steps, C_WIDTH), 0, 1).reshape(rows, C_WIDTH)
    ub = u.astype(BF16)
    for m in range(S5_SPLIT):
        bu = _mm(ub[:, m * S5_CH:(m + 1) * S5_CH], bblk_ref[m])
        xs_ref[:, m * S5_ST:(m + 1) * S5_ST] = bu[:, :S5_ST]
        xs_ref[:, S_LANES + m * S5_ST:S_LANES + (m + 1) * S5_ST] = bu[:, S5_ST:]

    for lb in range(S_LANES // SCAN_LANES):
        re = slice(lb * SCAN_LANES, (lb + 1) * SCAN_LANES)
        im = slice(S_LANES + lb * SCAN_LANES, S_LANES + (lb + 1) * SCAN_LANES)
        ar = jnp.broadcast_to(ar_ref[:, re], (BATCH, SCAN_LANES))
        ai = jnp.broadcast_to(ai_ref[:, re], (BATCH, SCAN_LANES))

        def step(t, carry):
            sr, si = carry
            r = pl.ds(pl.multiple_of(t * BATCH, BATCH), BATCH)
            nr = ar * sr - ai * si + xs_ref[r, re]
            ni = ar * si + ai * sr + xs_ref[r, im]
            xs_ref[r, re] = nr
            xs_ref[r, im] = ni
            return nr, ni

        sr, si = lax.fori_loop(0, steps, step, (st_ref[:, re], st_ref[:, im]),
                               unroll=SCAN_UNROLL)
        st_ref[:, re] = sr
        st_ref[:, im] = si

    ys = []
    for m in range(S5_SPLIT):
        xr = xs_ref[:, m * S5_ST:(m + 1) * S5_ST].astype(BF16)
        xi = xs_ref[:, S_LANES + m * S5_ST:S_LANES + (m + 1) * S5_ST].astype(BF16)
        ys.append(_mm(xr, cblk_ref[m, 0:S5_ST, :]) + _mm(xi, cblk_ref[m, S5_ST:2 * S5_ST, :]))
    y = jnp.concatenate(ys, axis=1) + d_ref[...] * u
    y_bt = jnp.swapaxes(y.reshape(steps, BATCH, C_WIDTH), 0, 1).reshape(rows, C_WIDTH)
    gy = jax.nn.gelu(y_bt).astype(BF16)
    outs = []
    for c in range(0, D_MODEL, MXU_TILE):
        val = _mm(gy, wout_ref[:, c:c + MXU_TILE])
        gate = _mm(gy, wout_ref[:, D_MODEL + c:D_MODEL + c + MXU_TILE])
        xc = x_ref[:, :, c:c + MXU_TILE].reshape(rows, MXU_TILE)
        outs.append(xc + val * jax.nn.sigmoid(gate))
    o_ref[...] = jnp.concatenate(outs, axis=1).reshape(BATCH, steps, D_MODEL)


def _s5_discretize(lam_re, lam_im, log_dt, b_re, b_im, c_re, c_im):
    dt = jnp.exp(log_dt)[:, None]
    mag = jnp.exp(lam_re * dt)
    ar = mag * jnp.cos(lam_im * dt)
    ai = mag * jnp.sin(lam_im * dt)
    den = lam_re * lam_re + lam_im * lam_im
    qr = ((ar - 1.0) * lam_re + ai * lam_im) / den
    qi = (ai * lam_re - (ar - 1.0) * lam_im) / den
    bbr = qr[..., None] * b_re - qi[..., None] * b_im
    bbi = qr[..., None] * b_im + qi[..., None] * b_re
    gs = C_GROUPS // S5_SPLIT
    eye = jnp.eye(gs, dtype=F32)
    split = lambda a: a.reshape((S5_SPLIT, gs) + a.shape[1:])
    blk_b = lambda a: jnp.einsum('mgpc,gh->mgchp', split(a), eye).reshape(S5_SPLIT, S5_CH, S5_ST)
    blk_c = lambda a: jnp.einsum('mgcp,gh->mgphc', split(a), eye).reshape(S5_SPLIT, S5_ST, S5_CH)
    bblk = jnp.concatenate([blk_b(bbr), blk_b(bbi)], axis=2)
    cblk = jnp.concatenate([blk_c(c_re), -blk_c(c_im)], axis=1)
    return (bblk.astype(BF16), ar.reshape(1, S_LANES), ai.reshape(1, S_LANES), cblk.astype(BF16))


def _odd_layer(x, norm, w_in_all, ssm, d_skip, w_out_all, layer):
    bblk, ar, ai, cblk = ssm
    spec = pl.BlockSpec((BATCH, SCAN_STEPS, D_MODEL), lambda i: (0, i, 0))
    hbm = pl.BlockSpec(memory_space=pl.ANY)
    return pl.pallas_call(
        functools.partial(_odd_kernel, layer=layer),
        grid=(SEQ // SCAN_STEPS,),
        in_specs=[
            spec,
            _resident((1, D_MODEL)),
            hbm,
            _resident((S5_SPLIT, S5_CH, 2 * S5_ST)),
            _resident((1, S_LANES)),
            _resident((1, S_LANES)),
            _resident((S5_SPLIT, 2 * S5_ST, S5_CH)),
            _resident((1, C_WIDTH)),
            hbm,
        ],
        out_specs=spec,
        out_shape=jax.ShapeDtypeStruct((BATCH, SEQ, D_MODEL), F32),
        scratch_shapes=[pltpu.VMEM((SCAN_STEPS * BATCH, 2 * S_LANES), F32),
                        pltpu.VMEM((BATCH, 2 * S_LANES), F32),
                        pltpu.VMEM((D_MODEL, C_WIDTH), BF16),
                        pltpu.VMEM((C_WIDTH, 2 * D_MODEL), BF16),
                        _stage(CAST_ROWS, C_WIDTH),
                        _stage(CAST_ROWS_WIDE, 2 * D_MODEL),
                        pltpu.SemaphoreType.DMA((CAST_SLOTS,))],
        compiler_params=_params(1),
        name="odd_mixer",
    )(x, norm.reshape(1, -1), w_in_all, bblk, ar, ai, cblk, d_skip.reshape(1, -1), w_out_all)


def _ca_kernel(x_ref, m_ref, g_ref, gm_ref, wq_hbm, wk_hbm, wv_hbm, wo_hbm, o_ref,
               wq_ref, wk_ref, wv_ref, wo_ref, kt_ref, v_ref, stage, sem, *, layer):
    @pl.when((pl.program_id(0) == 0) & (pl.program_id(1) == 0))
    def _():
        _load_bf16(wq_hbm.at[layer], wq_ref, stage, sem, scale=CA_HEAD_DIM ** -0.5)
        _load_bf16(wk_hbm.at[layer], wk_ref, stage, sem)
        _load_bf16(wv_hbm.at[layer], wv_ref, stage, sem)
        _load_bf16(wo_hbm.at[layer], wo_ref, stage, sem)

    @pl.when(pl.program_id(1) == 0)
    def _():
        mn = _rms(m_ref[...], gm_ref[...]).astype(BF16)
        kt_ref[...] = _mm(mn, wk_ref[...]).T.astype(BF16)
        v_ref[...] = _mm(mn, wv_ref[...]).astype(BF16)

    rows = CA_TILE // CA_SPLIT
    for blk in range(CA_SPLIT):
        rs = slice(blk * rows, (blk + 1) * rows)
        x = x_ref[rs, :]
        q = _mm(_rms(x, g_ref[...]).astype(BF16), wq_ref[...]).astype(BF16)
        heads = []
        for h in range(CA_HEADS):
            hs = slice(h * CA_HEAD_DIM, (h + 1) * CA_HEAD_DIM)
            sc = _mm(q[:, hs], kt_ref[hs, :])
            e = jnp.exp(sc - jnp.max(sc, axis=-1, keepdims=True))
            heads.append(_mm(e.astype(BF16), v_ref[:, hs]) / jnp.sum(e, axis=-1, keepdims=True))
        o = jnp.concatenate(heads, axis=1).astype(BF16)
        o_ref[rs, :] = x_ref[rs, :] + _mm(o, wo_ref[...])


def _cross_attention(x, mem, norm, mem_norm, wq_all, wk_all, wv_all, wo_all, layer):
    mem_spec = pl.BlockSpec((None, MEM_LEN, D_MODEL), lambda b, s: (b, 0, 0))
    hbm = pl.BlockSpec(memory_space=pl.ANY)
    return pl.pallas_call(
        functools.partial(_ca_kernel, layer=layer),
        grid=(BATCH, SEQ // CA_TILE),
        in_specs=[_tok_spec(CA_TILE), mem_spec, _resident((1, D_MODEL)), _resident((1, D_MODEL)),
                  hbm, hbm, hbm, hbm],
        out_specs=_tok_spec(CA_TILE),
        out_shape=jax.ShapeDtypeStruct((BATCH, SEQ, D_MODEL), F32),
        scratch_shapes=([pltpu.VMEM((D_MODEL, D_MODEL), BF16)] * 4
                        + [pltpu.VMEM((D_MODEL, MEM_LEN), BF16),
                           pltpu.VMEM((MEM_LEN, D_MODEL), BF16),
                           _stage(CAST_ROWS, D_MODEL),
                           pltpu.SemaphoreType.DMA((CAST_SLOTS,))]),
        compiler_params=_params(2),
        name="cross_attention",
    )(x, mem, norm.reshape(1, -1), mem_norm.reshape(1, -1), wq_all, wk_all, wv_all, wo_all)


def _ffn_kernel(x_ref, g_ref, wg_hbm, wu_hbm, wd_hbm, fg_ref, o_ref,
                wg_ref, wu_ref, wd_ref, stage_in, stage_out, sem, *, layer, final):
    @pl.when((pl.program_id(0) == 0) & (pl.program_id(1) == 0))
    def _():
        _load_bf16(wg_hbm.at[layer], wg_ref, stage_in, sem)
        _load_bf16(wu_hbm.at[layer], wu_ref, stage_in, sem)
        _load_bf16(wd_hbm.at[layer], wd_ref, stage_out, sem)

    xn = _rms(x_ref[...], g_ref[...]).astype(BF16)
    hs = []
    for n in range(FFN_HIDDEN // MXU_TILE):
        cs = slice(n * MXU_TILE, (n + 1) * MXU_TILE)
        hs.append((jax.nn.silu(_mm(xn, wg_ref[:, cs])) * _mm(xn, wu_ref[:, cs])).astype(BF16))
    y = x_ref[...] + _mm(jnp.concatenate(hs, axis=1), wd_ref[...])
    if final:
        y = _rms(y, fg_ref[...])
    o_ref[...] = y


def _ffn(x, norm, wg_all, wu_all, wd_all, final_norm, layer, final):
    hbm = pl.BlockSpec(memory_space=pl.ANY)
    return pl.pallas_call(
        functools.partial(_ffn_kernel, layer=layer, final=final),
        grid=(BATCH, SEQ // FFN_TILE),
        in_specs=[_tok_spec(FFN_TILE), _resident((1, D_MODEL)), hbm, hbm, hbm,
                  _resident((1, D_MODEL))],
        out_specs=_tok_spec(FFN_TILE),
        out_shape=jax.ShapeDtypeStruct((BATCH, SEQ, D_MODEL), F32),
        scratch_shapes=[pltpu.VMEM((D_MODEL, FFN_HIDDEN), BF16),
                        pltpu.VMEM((D_MODEL, FFN_HIDDEN), BF16),
                        pltpu.VMEM((FFN_HIDDEN, D_MODEL), BF16),
                        _stage(CAST_ROWS_WIDE, FFN_HIDDEN),
                        _stage(CAST_ROWS, D_MODEL),
                        pltpu.SemaphoreType.DMA((CAST_SLOTS,))],
        compiler_params=_params(2),
        name="ffn_final" if final else "ffn",
    )(x, norm.reshape(1, -1), wg_all, wu_all, wd_all, final_norm.reshape(1, -1))


def kernel(x, mem, e_norm, e_w_in, e_gmlp_w, e_gmlp_b, e_conv_w, e_conv_b, e_conv_ln_g, e_conv_ln_b, e_w_out, o_norm, o_w_in, o_lam_re, o_lam_im, o_log_dt, o_b_re, o_b_im, o_c_re, o_c_im, o_d, o_w_out, ca_norm, ca_mem_norm, ca_wq, ca_wk, ca_wv, ca_wo, ffn_norm, ffn_w_gate, ffn_w_up, ffn_w_down, final_norm):
    for i in range(DEPTH):
        j = i // 2
        if i % 2 == 0:
            x = _even_layer(x, e_norm[j], e_w_in, e_gmlp_w[j], e_gmlp_b[j], e_conv_w[j],
                            e_conv_b[j], e_conv_ln_g[j], e_conv_ln_b[j], e_w_out, layer=j)
        else:
            ssm = _s5_discretize(o_lam_re[j], o_lam_im[j], o_log_dt[j], o_b_re[j], o_b_im[j],
                                 o_c_re[j], o_c_im[j])
            x = _odd_layer(x, o_norm[j], o_w_in, ssm, o_d[j], o_w_out, layer=j)
        x = _cross_attention(x, mem, ca_norm[i], ca_mem_norm[i], ca_wq, ca_wk, ca_wv, ca_wo,
                             layer=i)
        x = _ffn(x, ffn_norm[i], ffn_w_gate, ffn_w_up, ffn_w_down, final_norm,
                 layer=i, final=(i == DEPTH - 1))
    return x
```

```python
import functools

import jax
import jax.numpy as jnp
from jax import lax
from jax.experimental import pallas as pl
from jax.experimental.pallas import tpu as pltpu

F32 = jnp.float32
BF16 = jnp.bfloat16

D_MODEL = 1024
BATCH = 8
SEQ = 4096
DEPTH = 2
CHUNK = 64
CHUNK_SHIFT = CHUNK.bit_length() - 1
MEM_LEN = 256
EPS = 1e-6
A_WIDTH = 512
A_GROUPS = 4
A_GROUP_DIM = A_WIDTH // A_GROUPS
GMLP_BLOCK = 128
B_WIDTH = 512
CONV_WIDTH = 31
MIX_WIDTH = A_WIDTH + B_WIDTH
IN_WIDTH = 2 * A_WIDTH + 2 * B_WIDTH
C_WIDTH = 512
C_GROUP_CH = 16
C_GROUPS = C_WIDTH // C_GROUP_CH
C_STATE = 64
S_LANES = C_GROUPS * C_STATE
S5_SPLIT = 2
S5_CH = C_WIDTH // S5_SPLIT
S5_ST = S_LANES // S5_SPLIT
CA_HEADS = 4
CA_HEAD_DIM = D_MODEL // CA_HEADS
FFN_HIDDEN = -(-8 * D_MODEL // (3 * 256)) * 256

VMEM_LIMIT_BYTES = 56 * 1024 * 1024
SUBLANES = 8
LANES = 128
MXU_TILE = 256

EVEN_TILE = 1024
CA_TILE = 2048
CA_SPLIT = 4
FFN_TILE = 1024
CONV_HALO = 32
CONV_ROWS = 128
CONV_UNROLL = 8
SCAN_STEPS = 128
SCAN_LANES = 1024
SCAN_UNROLL = 16
CAST_SLOTS = 3
CAST_ROWS = 256
CAST_ROWS_WIDE = 128
assert FFN_HIDDEN % MXU_TILE == 0 and D_MODEL % MXU_TILE == 0
assert CONV_WIDTH - 1 <= CONV_HALO and CONV_HALO % SUBLANES == 0


def _params(n_axes):
    return pltpu.CompilerParams(
        dimension_semantics=("arbitrary",) * n_axes,
        vmem_limit_bytes=VMEM_LIMIT_BYTES)


def _resident(shape):
    return pl.BlockSpec(shape, lambda *_: (0,) * len(shape),
                        pipeline_mode=pl.Buffered(1))


def _tok_spec(tile):
    return pl.BlockSpec((None, tile, D_MODEL), lambda b, s: (b, s, 0))


def _rms(x, g):
    return x * lax.rsqrt(jnp.mean(x * x, axis=-1, keepdims=True) + EPS) * g


def _ln(x):
    mu = jnp.mean(x, axis=-1, keepdims=True)
    xc = x - mu
    return xc * lax.rsqrt(jnp.mean(xc * xc, axis=-1, keepdims=True) + EPS)


def _mm(a, b):
    return jnp.dot(a, b, preferred_element_type=F32)


def _load_bf16(src, dst_ref, stage_ref, sem_ref, scale=None):
    slots, chunk = stage_ref.shape[0], stage_ref.shape[1]
    n = dst_ref.shape[0] // chunk

    def copy(j):
        return pltpu.make_async_copy(src.at[pl.ds(j * chunk, chunk), :],
                                     stage_ref.at[j % slots], sem_ref.at[j % slots])

    for j in range(min(slots - 1, n)):
        copy(j).start()
    for j in range(n):
        if j + slots - 1 < n:
            copy(j + slots - 1).start()
        copy(j).wait()
        w = stage_ref[j % slots]
        dst_ref[j * chunk:(j + 1) * chunk, :] = (w if scale is None else w * scale).astype(BF16)


def _stage(chunk_rows, cols):
    return pltpu.VMEM((CAST_SLOTS, chunk_rows, cols), F32)


def _even_kernel(x_ref, g_ref, win_hbm, gw_ref, gb_ref, cw_ref, cb_ref, lng_ref,
                 lnb_ref, wout_hbm, o_ref, hext_ref, conv_ref, win_ref, wout_ref,
                 stage_in, stage_out, sem, *, layer):
    tm = EVEN_TILE
    nblk = tm // GMLP_BLOCK

    @pl.when((pl.program_id(0) == 0) & (pl.program_id(1) == 0))
    def _():
        _load_bf16(win_hbm.at[layer], win_ref, stage_in, sem)
        _load_bf16(wout_hbm.at[layer], wout_ref, stage_out, sem)

    x = x_ref[...]
    hn = _rms(x, g_ref[...]).astype(BF16)
    proj = _mm(hn, win_ref[...])

    u = jax.nn.gelu(proj[:, :A_WIDTH])
    v = _ln(jax.nn.gelu(proj[:, A_WIDTH:2 * A_WIDTH])).astype(BF16)
    ri = lax.broadcasted_iota(jnp.int32, (GMLP_BLOCK, GMLP_BLOCK), 0) >> CHUNK_SHIFT
    ci = lax.broadcasted_iota(jnp.int32, (GMLP_BLOCK, GMLP_BLOCK), 1) >> CHUNK_SHIFT
    keep = ci <= ri
    per_group = []
    for g in range(A_GROUPS):
        gs = slice(g * A_GROUP_DIM, (g + 1) * A_GROUP_DIM)
        rhs = jnp.concatenate(
            [v[nb * GMLP_BLOCK:(nb + 1) * GMLP_BLOCK, gs] for nb in range(nblk)], axis=1)
        wm = jnp.where(keep, gw_ref[g], 0.0).astype(BF16)
        per_group.append(_mm(wm, rhs))
    sg = jnp.concatenate(
        [jnp.concatenate([per_group[g][:, nb * A_GROUP_DIM:(nb + 1) * A_GROUP_DIM]
                          for g in range(A_GROUPS)], axis=1) for nb in range(nblk)], axis=0)
    bias = jnp.concatenate([gb_ref[...]] * nblk, axis=0)
    out_a = u * (sg + bias)

    h = proj[:, 2 * A_WIDTH:2 * A_WIDTH + B_WIDTH] * jax.nn.sigmoid(proj[:, 2 * A_WIDTH + B_WIDTH:])

    lane_blocks = B_WIDTH // LANES
    row_blocks = tm // CONV_ROWS

    @pl.when(pl.program_id(1) == 0)
    def _():
        hext_ref[:, 0:CONV_HALO, :] = jnp.zeros((lane_blocks, CONV_HALO, LANES), F32)

    for cb in range(lane_blocks):
        hext_ref[cb, CONV_HALO:CONV_HALO + tm, :] = h[:, cb * LANES:(cb + 1) * LANES]

    def conv_blocks(it, carry):
        for j in range(CONV_UNROLL):
            blk = it * CONV_UNROLL + j
            cb = blk // row_blocks
            r0 = pl.multiple_of((blk % row_blocks) * CONV_ROWS, CONV_ROWS)
            win = hext_ref[cb, pl.ds(r0, CONV_ROWS + CONV_HALO), :]
            acc = jnp.zeros((CONV_ROWS, LANES), F32)
            for r in range(SUBLANES):
                shifted = win if r == 0 else pltpu.roll(win, r, axis=0)
                for q in range(CONV_HALO // SUBLANES):
                    lag = SUBLANES * q + r
                    if lag < CONV_WIDTH:
                        k = CONV_WIDTH - 1 - lag
                        lo = CONV_HALO - SUBLANES * q
                        acc = acc + cw_ref[cb, k:k + 1, :] * shifted[lo:lo + CONV_ROWS, :]
            conv_ref[cb, pl.ds(r0, CONV_ROWS), :] = acc
        return carry

    lax.fori_loop(0, lane_blocks * row_blocks // CONV_UNROLL, conv_blocks, 0)
    for cb in range(lane_blocks):
        hext_ref[cb, 0:CONV_HALO, :] = hext_ref[cb, tm:tm + CONV_HALO, :]
    conv = jnp.concatenate([conv_ref[cb] for cb in range(lane_blocks)], axis=1)
    c = _ln(conv + cb_ref[...]) * lng_ref[...] + lnb_ref[...]
    out_b = jax.nn.silu(c)

    mix = (_mm(out_a.astype(BF16), wout_ref[0:A_WIDTH, :])
           + _mm(out_b.astype(BF16), wout_ref[A_WIDTH:MIX_WIDTH, :]))
    o_ref[...] = x_ref[...] + mix


def _even_layer(x, norm, w_in_all, gmlp_w, gmlp_b, conv_w, conv_b, ln_g, ln_b, w_out_all, layer):
    gb_full = jnp.repeat(gmlp_b.T, A_GROUP_DIM, axis=1)
    row = lambda a: a.reshape(1, -1)
    hbm = pl.BlockSpec(memory_space=pl.ANY)
    lane_blocks = B_WIDTH // LANES
    conv_w_b = conv_w.reshape(CONV_WIDTH, lane_blocks, LANES).transpose(1, 0, 2)
    return pl.pallas_call(
        functools.partial(_even_kernel, layer=layer),
        grid=(BATCH, SEQ // EVEN_TILE),
        in_specs=[
            _tok_spec(EVEN_TILE),
            _resident((1, D_MODEL)),
            hbm,
            _resident((A_GROUPS, GMLP_BLOCK, GMLP_BLOCK)),
            _resident((GMLP_BLOCK, A_WIDTH)),
            _resident((lane_blocks, CONV_WIDTH, LANES)),
            _resident((1, B_WIDTH)),
            _resident((1, B_WIDTH)),
            _resident((1, B_WIDTH)),
            hbm,
        ],
        out_specs=_tok_spec(EVEN_TILE),
        out_shape=jax.ShapeDtypeStruct((BATCH, SEQ, D_MODEL), F32),
        scratch_shapes=[pltpu.VMEM((lane_blocks, CONV_HALO + EVEN_TILE, LANES), F32),
                        pltpu.VMEM((lane_blocks, EVEN_TILE, LANES), F32),
                        pltpu.VMEM((D_MODEL, IN_WIDTH), BF16),
                        pltpu.VMEM((MIX_WIDTH, D_MODEL), BF16),
                        _stage(CAST_ROWS_WIDE, IN_WIDTH),
                        _stage(CAST_ROWS, D_MODEL),
                        pltpu.SemaphoreType.DMA((CAST_SLOTS,))],
        compiler_params=_params(2),
        name="even_mixer",
    )(x, row(norm), w_in_all, gmlp_w, gb_full, conv_w_b, row(conv_b), row(ln_g),
      row(ln_b), w_out_all)


def _odd_kernel(x_ref, g_ref, win_hbm, bblk_ref, ar_ref, ai_ref, cblk_ref, d_ref, wout_hbm,
                o_ref, xs_ref, st_ref, win_ref, wout_ref, stage_in, stage_out, sem, *, layer):
    steps = SCAN_STEPS
    rows = steps * BATCH

    @pl.when(pl.program_id(0) == 0)
    def _():
        st_ref[...] = jnp.zeros(st_ref.shape, F32)
        _load_bf16(win_hbm.at[layer], win_ref, stage_in, sem)
        _load_bf16(wout_hbm.at[layer], wout_ref, stage_out, sem)

    x = x_ref[...].reshape(rows, D_MODEL)
    u_bt = _mm(_rms(x, g_ref[...]).astype(BF16), win_ref[...])
    u = jnp.swapaxes(u_bt.reshape(BATCH, steps, C_WIDTH), 0, 1).reshape(rows, C_WIDTH)
    ub = u.astype(BF16)
    for m in range(S5_SPLIT):
        bu = _mm(ub[:, m * S5_CH:(m + 1) * S5_CH], bblk_ref[m])
        xs_ref[:, m * S5_ST:(m + 1) * S5_ST] = bu[:, :S5_ST]
        xs_ref[:, S_LANES + m * S5_ST:S_LANES + (m + 1) * S5_ST] = bu[:, S5_ST:]

    for lb in range(S_LANES // SCAN_LANES):
        re = slice(lb * SCAN_LANES, (lb + 1) * SCAN_LANES)
        im = slice(S_LANES + lb * SCAN_LANES, S_LANES + (lb + 1) * SCAN_LANES)
        ar = jnp.broadcast_to(ar_ref[:, re], (BATCH, SCAN_LANES))
        ai = jnp.broadcast_to(ai_ref[:, re], (BATCH, SCAN_LANES))

        def step(t, carry):
            sr, si = carry
            r = pl.ds(pl.multiple_of(t * BATCH, BATCH), BATCH)
            nr = ar * sr - ai * si + xs_ref[r, re]
            ni = ar * si + ai * sr + xs_ref[r, im]
            xs_ref[r, re] = nr
            xs_ref[r, im] = ni
            return nr, ni

        sr, si = lax.fori_loop(0, steps, step, (st_ref[:, re], st_ref[:, im]),
                               unroll=SCAN_UNROLL)
        st_ref[:, re] = sr
        st_ref[:, im] = si

    ys = []
    for m in range(S5_SPLIT):
        xr = xs_ref[:, m * S5_ST:(m + 1) * S5_ST].astype(BF16)
        xi = xs_ref[:, S_LANES + m * S5_ST:S_LANES + (m + 1) * S5_ST].astype(BF16)
        ys.append(_mm(xr, cblk_ref[m, 0:S5_ST, :]) + _mm(xi, cblk_ref[m, S5_ST:2 * S5_ST, :]))
    y = jnp.concatenate(ys, axis=1) + d_ref[...] * u
    y_bt = jnp.swapaxes(y.reshape(steps, BATCH, C_WIDTH), 0, 1).reshape(rows, C_WIDTH)
    gy = jax.nn.gelu(y_bt).astype(BF16)
    outs = []
    for c in range(0, D_MODEL, MXU_TILE):
        val = _mm(gy, wout_ref[:, c:c + MXU_TILE])
        gate = _mm(gy, wout_ref[:, D_MODEL + c:D_MODEL + c + MXU_TILE])
        xc = x_ref[:, :, c:c + MXU_TILE].reshape(rows, MXU_TILE)
        outs.append(xc + val * jax.nn.sigmoid(gate))
    o_ref[...] = jnp.concatenate(outs, axis=1).reshape(BATCH, steps, D_MODEL)


def _s5_discretize(lam_re, lam_im, log_dt, b_re, b_im, c_re, c_im):
    dt = jnp.exp(log_dt)[:, None]
    mag = jnp.exp(lam_re * dt)
    ar = mag * jnp.cos(lam_im * dt)
    ai = mag * jnp.sin(lam_im * dt)
    den = lam_re * lam_re + lam_im * lam_im
    qr = ((ar - 1.0) * lam_re + ai * lam_im) / den
    qi = (ai * lam_re - (ar - 1.0) * lam_im) / den
    bbr = qr[..., None] * b_re - qi[..., None] * b_im
    bbi = qr[..., None] * b_im + qi[..., None] * b_re
    gs = C_GROUPS // S5_SPLIT
    eye = jnp.eye(gs, dtype=F32)
    split = lambda a: a.reshape((S5_SPLIT, gs) + a.shape[1:])
    blk_b = lambda a: jnp.einsum('mgpc,gh->mgchp', split(a), eye).reshape(S5_SPLIT, S5_CH, S5_ST)
    blk_c = lambda a: jnp.einsum('mgcp,gh->mgphc', split(a), eye).reshape(S5_SPLIT, S5_ST, S5_CH)
    bblk = jnp.concatenate([blk_b(bbr), blk_b(bbi)], axis=2)
    cblk = jnp.concatenate([blk_c(c_re), -blk_c(c_im)], axis=1)
    return (bblk.astype(BF16), ar.reshape(1, S_LANES), ai.reshape(1, S_LANES), cblk.astype(BF16))


def _odd_layer(x, norm, w_in_all, ssm, d_skip, w_out_all, layer):
    bblk, ar, ai, cblk = ssm
    spec = pl.BlockSpec((BATCH, SCAN_STEPS, D_MODEL), lambda i: (0, i, 0))
    hbm = pl.BlockSpec(memory_space=pl.ANY)
    return pl.pallas_call(
        functools.partial(_odd_kernel, layer=layer),
        grid=(SEQ // SCAN_STEPS,),
        in_specs=[
            spec,
            _resident((1, D_MODEL)),
            hbm,
            _resident((S5_SPLIT, S5_CH, 2 * S5_ST)),
            _resident((1, S_LANES)),
            _resident((1, S_LANES)),
            _resident((S5_SPLIT, 2 * S5_ST, S5_CH)),
            _resident((1, C_WIDTH)),
            hbm,
        ],
        out_specs=spec,
        out_shape=jax.ShapeDtypeStruct((BATCH, SEQ, D_MODEL), F32),
        scratch_shapes=[pltpu.VMEM((SCAN_STEPS * BATCH, 2 * S_LANES), F32),
                        pltpu.VMEM((BATCH, 2 * S_LANES), F32),
                        pltpu.VMEM((D_MODEL, C_WIDTH), BF16),
                        pltpu.VMEM((C_WIDTH, 2 * D_MODEL), BF16),
                        _stage(CAST_ROWS, C_WIDTH),
                        _stage(CAST_ROWS_WIDE, 2 * D_MODEL),
                        pltpu.SemaphoreType.DMA((CAST_SLOTS,))],
        compiler_params=_params(1),
        name="odd_mixer",
    )(x, norm.reshape(1, -1), w_in_all, bblk, ar, ai, cblk, d_skip.reshape(1, -1), w_out_all)


def _ca_kernel(x_ref, m_ref, g_ref, gm_ref, wq_hbm, wk_hbm, wv_hbm, wo_hbm, o_ref,
               wq_ref, wk_ref, wv_ref, wo_ref, kt_ref, v_ref, stage, sem, *, layer):
    @pl.when((pl.program_id(0) == 0) & (pl.program_id(1) == 0))
    def _():
        _load_bf16(wq_hbm.at[layer], wq_ref, stage, sem, scale=CA_HEAD_DIM ** -0.5)
        _load_bf16(wk_hbm.at[layer], wk_ref, stage, sem)
        _load_bf16(wv_hbm.at[layer], wv_ref, stage, sem)
        _load_bf16(wo_hbm.at[layer], wo_ref, stage, sem)

    @pl.when(pl.program_id(1) == 0)
    def _():
        mn = _rms(m_ref[...], gm_ref[...]).astype(BF16)
        kt_ref[...] = _mm(mn, wk_ref[...]).T.astype(BF16)
        v_ref[...] = _mm(mn, wv_ref[...]).astype(BF16)

    rows = CA_TILE // CA_SPLIT
    for blk in range(CA_SPLIT):
        rs = slice(blk * rows, (blk + 1) * rows)
        x = x_ref[rs, :]
        q = _mm(_rms(x, g_ref[...]).astype(BF16), wq_ref[...]).astype(BF16)
        heads = []
        for h in range(CA_HEADS):
            hs = slice(h * CA_HEAD_DIM, (h + 1) * CA_HEAD_DIM)
            sc = _mm(q[:, hs], kt_ref[hs, :])
            e = jnp.exp(sc - jnp.max(sc, axis=-1, keepdims=True))
            heads.append(_mm(e.astype(BF16), v_ref[:, hs]) / jnp.sum(e, axis=-1, keepdims=True))
        o = jnp.concatenate(heads, axis=1).astype(BF16)
        o_ref[rs, :] = x_ref[rs, :] + _mm(o, wo_ref[...])


def _cross_attention(x, mem, norm, mem_norm, wq_all, wk_all, wv_all, wo_all, layer):
    mem_spec = pl.BlockSpec((None, MEM_LEN, D_MODEL), lambda b, s: (b, 0, 0))
    hbm = pl.BlockSpec(memory_space=pl.ANY)
    return pl.pallas_call(
        functools.partial(_ca_kernel, layer=layer),
        grid=(BATCH, SEQ // CA_TILE),
        in_specs=[_tok_spec(CA_TILE), mem_spec, _resident((1, D_MODEL)), _resident((1, D_MODEL)),
                  hbm, hbm, hbm, hbm],
        out_specs=_tok_spec(CA_TILE),
        out_shape=jax.ShapeDtypeStruct((BATCH, SEQ, D_MODEL), F32),
        scratch_shapes=([pltpu.VMEM((D_MODEL, D_MODEL), BF16)] * 4
                        + [pltpu.VMEM((D_MODEL, MEM_LEN), BF16),
                           pltpu.VMEM((MEM_LEN, D_MODEL), BF16),
                           _stage(CAST_ROWS, D_MODEL),
                           pltpu.SemaphoreType.DMA((CAST_SLOTS,))]),
        compiler_params=_params(2),
        name="cross_attention",
    )(x, mem, norm.reshape(1, -1), mem_norm.reshape(1, -1), wq_all, wk_all, wv_all, wo_all)


def _ffn_kernel(x_ref, g_ref, wg_hbm, wu_hbm, wd_hbm, fg_ref, o_ref,
                wg_ref, wu_ref, wd_ref, stage_in, stage_out, sem, *, layer, final):
    @pl.when((pl.program_id(0) == 0) & (pl.program_id(1) == 0))
    def _():
        _load_bf16(wg_hbm.at[layer], wg_ref, stage_in, sem)
        _load_bf16(wu_hbm.at[layer], wu_ref, stage_in, sem)
        _load_bf16(wd_hbm.at[layer], wd_ref, stage_out, sem)

    xn = _rms(x_ref[...], g_ref[...]).astype(BF16)
    hs = []
    for n in range(FFN_HIDDEN // MXU_TILE):
        cs = slice(n * MXU_TILE, (n + 1) * MXU_TILE)
        hs.append((jax.nn.silu(_mm(xn, wg_ref[:, cs])) * _mm(xn, wu_ref[:, cs])).astype(BF16))
    y = x_ref[...] + _mm(jnp.concatenate(hs, axis=1), wd_ref[...])
    if final:
        y = _rms(y, fg_ref[...])
    o_ref[...] = y


def _ffn(x, norm, wg_all, wu_all, wd_all, final_norm, layer, final):
    hbm = pl.BlockSpec(memory_space=pl.ANY)
    return pl.pallas_call(
        functools.partial(_ffn_kernel, layer=layer, final=final),
        grid=(BATCH, SEQ // FFN_TILE),
        in_specs=[_tok_spec(FFN_TILE), _resident((1, D_MODEL)), hbm, hbm, hbm,
                  _resident((1, D_MODEL))],
        out_specs=_tok_spec(FFN_TILE),
        out_shape=jax.ShapeDtypeStruct((BATCH, SEQ, D_MODEL), F32),
        scratch_shapes=[pltpu.VMEM((D_MODEL, FFN_HIDDEN), BF16),
                        pltpu.VMEM((D_MODEL, FFN_HIDDEN), BF16),
                        pltpu.VMEM((FFN_HIDDEN, D_MODEL), BF16),
                        _stage(CAST_ROWS_WIDE, FFN_HIDDEN),
                        _stage(CAST_ROWS, D_MODEL),
                        pltpu.SemaphoreType.DMA((CAST_SLOTS,))],
        compiler_params=_params(2),
        name="ffn_final" if final else "ffn",
    )(x, norm.reshape(1, -1), wg_all, wu_all, wd_all, final_norm.reshape(1, -1))


def kernel(x, mem, e_norm, e_w_in, e_gmlp_w, e_gmlp_b, e_conv_w, e_conv_b, e_conv_ln_g, e_conv_ln_b, e_w_out, o_norm, o_w_in, o_lam_re, o_lam_im, o_log_dt, o_b_re, o_b_im, o_c_re, o_c_im, o_d, o_w_out, ca_norm, ca_mem_norm, ca_wq, ca_wk, ca_wv, ca_wo, ffn_norm, ffn_w_gate, ffn_w_up, ffn_w_down, final_norm):
    for i in range(DEPTH):
        j = i // 2
        if i % 2 == 0:
            x = _even_layer(x, e_norm[j], e_w_in, e_gmlp_w[j], e_gmlp_b[j], e_conv_w[j],
                            e_conv_b[j], e_conv_ln_g[j], e_conv_ln_b[j], e_w_out, layer=j)
        else:
            ssm = _s5_discretize(o_lam_re[j], o_lam_im[j], o_log_dt[j], o_b_re[j], o_b_im[j],
                                 o_c_re[j], o_c_im[j])
            x = _odd_layer(x, o_norm[j], o_w_in, ssm, o_d[j], o_w_out, layer=j)
        x = _cross_attention(x, mem, ca_norm[i], ca_mem_norm[i], ca_wq, ca_wk, ca_wv, ca_wo,
                             layer=i)
        x = _ffn(x, ffn_norm[i], ffn_w_gate, ffn_w_up, ffn_w_down, final_norm,
                 layer=i, final=(i == DEPTH - 1))
    return x
```

```python
import functools

import jax
import jax.numpy as jnp
from jax import lax
from jax.experimental import pallas as pl
from jax.experimental.pallas import tpu as pltpu

F32 = jnp.float32
BF16 = jnp.bfloat16

D_MODEL = 1024
BATCH = 8
SEQ = 4096
DEPTH = 2
CHUNK = 64
CHUNK_SHIFT = CHUNK.bit_length() - 1
MEM_LEN = 256
EPS = 1e-6
A_WIDTH = 512
A_GROUPS = 4
A_GROUP_DIM = A_WIDTH // A_GROUPS
GMLP_BLOCK = 128
B_WIDTH = 512
CONV_WIDTH = 31
MIX_WIDTH = A_WIDTH + B_WIDTH
IN_WIDTH = 2 * A_WIDTH + 2 * B_WIDTH
C_WIDTH = 512
C_GROUP_CH = 16
C_GROUPS = C_WIDTH // C_GROUP_CH
C_STATE = 64
S_LANES = C_GROUPS * C_STATE
S5_SPLIT = 2
S5_CH = C_WIDTH // S5_SPLIT
S5_ST = S_LANES // S5_SPLIT
CA_HEADS = 4
CA_HEAD_DIM = D_MODEL // CA_HEADS
FFN_HIDDEN = -(-8 * D_MODEL // (3 * 256)) * 256

VMEM_LIMIT_BYTES = 56 * 1024 * 1024
SUBLANES = 8
LANES = 128
MXU_TILE = 256

EVEN_TILE = 1024
CA_TILE = 2048
CA_SPLIT = 4
FFN_TILE = 1024
CONV_HALO = 32
CONV_ROWS = 128
CONV_UNROLL = 8
SCAN_STEPS = 128
SCAN_LANES = 1024
SCAN_UNROLL = 16
CAST_SLOTS = 3
CAST_ROWS = 256
CAST_ROWS_WIDE = 128
assert FFN_HIDDEN % MXU_TILE == 0 and D_MODEL % MXU_TILE == 0
assert CONV_WIDTH - 1 <= CONV_HALO and CONV_HALO % SUBLANES == 0


def _params(n_axes):
    return pltpu.CompilerParams(
        dimension_semantics=("arbitrary",) * n_axes,
        vmem_limit_bytes=VMEM_LIMIT_BYTES)


def _resident(shape):
    return pl.BlockSpec(shape, lambda *_: (0,) * len(shape),
                        pipeline_mode=pl.Buffered(1))


def _tok_spec(tile):
    return pl.BlockSpec((None, tile, D_MODEL), lambda b, s: (b, s, 0))


def _rms(x, g):
    return x * lax.rsqrt(jnp.mean(x * x, axis=-1, keepdims=True) + EPS) * g


def _ln(x):
    mu = jnp.mean(x, axis=-1, keepdims=True)
    xc = x - mu
    return xc * lax.rsqrt(jnp.mean(xc * xc, axis=-1, keepdims=True) + EPS)


def _mm(a, b):
    return jnp.dot(a, b, preferred_element_type=F32)


def _load_bf16_many(weights, stage_ref, sem_ref):
    slots, chunk = stage_ref.shape[0], stage_ref.shape[1]
    chunks = [(src, dst, scale, r) for src, dst, scale in weights
              for r in range(0, dst.shape[0], chunk)]
    n = len(chunks)

    def copy(j):
        src, _, _, r = chunks[j]
        return pltpu.make_async_copy(src.at[pl.ds(r, chunk), :],
                                     stage_ref.at[j % slots], sem_ref.at[j % slots])

    for j in range(min(slots - 1, n)):
        copy(j).start()
    for j in range(n):
        if j + slots - 1 < n:
            copy(j + slots - 1).start()
        copy(j).wait()
        _, dst, scale, r = chunks[j]
        w = stage_ref[j % slots]
        dst[r:r + chunk, :] = (w if scale is None else w * scale).astype(BF16)


def _load_bf16(src, dst_ref, stage_ref, sem_ref, scale=None):
    _load_bf16_many([(src, dst_ref, scale)], stage_ref, sem_ref)


def _stage(chunk_rows, cols):
    return pltpu.VMEM((CAST_SLOTS, chunk_rows, cols), F32)


def _even_kernel(x_ref, g_ref, win_hbm, gw_ref, gb_ref, cw_ref, cb_ref, lng_ref,
                 lnb_ref, wout_hbm, o_ref, hext_ref, conv_ref, win_ref, wout_ref,
                 stage_in, stage_out, sem, *, layer):
    tm = EVEN_TILE
    nblk = tm // GMLP_BLOCK

    @pl.when((pl.program_id(0) == 0) & (pl.program_id(1) == 0))
    def _():
        _load_bf16(win_hbm.at[layer], win_ref, stage_in, sem)
        _load_bf16(wout_hbm.at[layer], wout_ref, stage_out, sem)

    x = x_ref[...]
    hn = _rms(x, g_ref[...]).astype(BF16)
    proj = _mm(hn, win_ref[...])

    u = jax.nn.gelu(proj[:, :A_WIDTH])
    v = _ln(jax.nn.gelu(proj[:, A_WIDTH:2 * A_WIDTH])).astype(BF16)
    ri = lax.broadcasted_iota(jnp.int32, (GMLP_BLOCK, GMLP_BLOCK), 0) >> CHUNK_SHIFT
    ci = lax.broadcasted_iota(jnp.int32, (GMLP_BLOCK, GMLP_BLOCK), 1) >> CHUNK_SHIFT
    keep = ci <= ri
    per_group = []
    for g in range(A_GROUPS):
        gs = slice(g * A_GROUP_DIM, (g + 1) * A_GROUP_DIM)
        rhs = jnp.concatenate(
            [v[nb * GMLP_BLOCK:(nb + 1) * GMLP_BLOCK, gs] for nb in range(nblk)], axis=1)
        wm = jnp.where(keep, gw_ref[g], 0.0).astype(BF16)
        per_group.append(_mm(wm, rhs))
    sg = jnp.concatenate(
        [jnp.concatenate([per_group[g][:, nb * A_GROUP_DIM:(nb + 1) * A_GROUP_DIM]
                          for g in range(A_GROUPS)], axis=1) for nb in range(nblk)], axis=0)
    bias = jnp.concatenate([gb_ref[...]] * nblk, axis=0)
    out_a = u * (sg + bias)

    h = proj[:, 2 * A_WIDTH:2 * A_WIDTH + B_WIDTH] * jax.nn.sigmoid(proj[:, 2 * A_WIDTH + B_WIDTH:])

    lane_blocks = B_WIDTH // LANES
    row_blocks = tm // CONV_ROWS

    @pl.when(pl.program_id(1) == 0)
    def _():
        hext_ref[:, 0:CONV_HALO, :] = jnp.zeros((lane_blocks, CONV_HALO, LANES), F32)

    for cb in range(lane_blocks):
        hext_ref[cb, CONV_HALO:CONV_HALO + tm, :] = h[:, cb * LANES:(cb + 1) * LANES]

    def conv_blocks(it, carry):
        for j in range(CONV_UNROLL):
            blk = it * CONV_UNROLL + j
            cb = blk // row_blocks
            r0 = pl.multiple_of((blk % row_blocks) * CONV_ROWS, CONV_ROWS)
            win = hext_ref[cb, pl.ds(r0, CONV_ROWS + CONV_HALO), :]
            acc = jnp.zeros((CONV_ROWS, LANES), F32)
            for r in range(SUBLANES):
                shifted = win if r == 0 else pltpu.roll(win, r, axis=0)
                for q in range(CONV_HALO // SUBLANES):
                    lag = SUBLANES * q + r
                    if lag < CONV_WIDTH:
                        k = CONV_WIDTH - 1 - lag
                        lo = CONV_HALO - SUBLANES * q
                        acc = acc + cw_ref[cb, k:k + 1, :] * shifted[lo:lo + CONV_ROWS, :]
            conv_ref[cb, pl.ds(r0, CONV_ROWS), :] = acc
        return carry

    lax.fori_loop(0, lane_blocks * row_blocks // CONV_UNROLL, conv_blocks, 0)
    for cb in range(lane_blocks):
        hext_ref[cb, 0:CONV_HALO, :] = hext_ref[cb, tm:tm + CONV_HALO, :]
    conv = jnp.concatenate([conv_ref[cb] for cb in range(lane_blocks)], axis=1)
    c = _ln(conv + cb_ref[...]) * lng_ref[...] + lnb_ref[...]
    out_b = jax.nn.silu(c)

    mix = (_mm(out_a.astype(BF16), wout_ref[0:A_WIDTH, :])
           + _mm(out_b.astype(BF16), wout_ref[A_WIDTH:MIX_WIDTH, :]))
    o_ref[...] = x_ref[...] + mix


def _even_layer(x, norm, w_in_all, gmlp_w, gmlp_b, conv_w, conv_b, ln_g, ln_b, w_out_all, layer):
    gb_full = jnp.repeat(gmlp_b.T, A_GROUP_DIM, axis=1)
    row = lambda a: a.reshape(1, -1)
    hbm = pl.BlockSpec(memory_space=pl.ANY)
    lane_blocks = B_WIDTH // LANES
    conv_w_b = conv_w.reshape(CONV_WIDTH, lane_blocks, LANES).transpose(1, 0, 2)
    return pl.pallas_call(
        functools.partial(_even_kernel, layer=layer),
        grid=(BATCH, SEQ // EVEN_TILE),
        in_specs=[
            _tok_spec(EVEN_TILE),
            _resident((1, D_MODEL)),
            hbm,
            _resident((A_GROUPS, GMLP_BLOCK, GMLP_BLOCK)),
            _resident((GMLP_BLOCK, A_WIDTH)),
            _resident((lane_blocks, CONV_WIDTH, LANES)),
            _resident((1, B_WIDTH)),
            _resident((1, B_WIDTH)),
            _resident((1, B_WIDTH)),
            hbm,
        ],
        out_specs=_tok_spec(EVEN_TILE),
        out_shape=jax.ShapeDtypeStruct((BATCH, SEQ, D_MODEL), F32),
        scratch_shapes=[pltpu.VMEM((lane_blocks, CONV_HALO + EVEN_TILE, LANES), F32),
                        pltpu.VMEM((lane_blocks, EVEN_TILE, LANES), F32),
                        pltpu.VMEM((D_MODEL, IN_WIDTH), BF16),
                        pltpu.VMEM((MIX_WIDTH, D_MODEL), BF16),
                        _stage(CAST_ROWS_WIDE, IN_WIDTH),
                        _stage(CAST_ROWS, D_MODEL),
                        pltpu.SemaphoreType.DMA((CAST_SLOTS,))],
        compiler_params=_params(2),
        name="even_mixer",
    )(x, row(norm), w_in_all, gmlp_w, gb_full, conv_w_b, row(conv_b), row(ln_g),
      row(ln_b), w_out_all)


def _odd_kernel(x_ref, g_ref, win_hbm, bblk_ref, ar_ref, ai_ref, cblk_ref, d_ref, wout_hbm,
                o_ref, xs_ref, st_ref, win_ref, wout_ref, stage_in, stage_out, sem, *, layer):
    steps = SCAN_STEPS
    rows = steps * BATCH

    @pl.when(pl.program_id(0) == 0)
    def _():
        st_ref[...] = jnp.zeros(st_ref.shape, F32)
        _load_bf16(win_hbm.at[layer], win_ref, stage_in, sem)
        _load_bf16(wout_hbm.at[layer], wout_ref, stage_out, sem)

    x = x_ref[...].reshape(rows, D_MODEL)
    u_bt = _mm(_rms(x, g_ref[...]).astype(BF16), win_ref[...])
    u = jnp.swapaxes(u_bt.reshape(BATCH, steps, C_WIDTH), 0, 1).reshape(rows, C_WIDTH)
    ub = u.astype(BF16)
    for m in range(S5_SPLIT):
        bu = _mm(ub[:, m * S5_CH:(m + 1) * S5_CH], bblk_ref[m])
        xs_ref[:, m * S5_ST:(m + 1) * S5_ST] = bu[:, :S5_ST]
        xs_ref[:, S_LANES + m * S5_ST:S_LANES + (m + 1) * S5_ST] = bu[:, S5_ST:]

    for lb in range(S_LANES // SCAN_LANES):
        re = slice(lb * SCAN_LANES, (lb + 1) * SCAN_LANES)
        im = slice(S_LANES + lb * SCAN_LANES, S_LANES + (lb + 1) * SCAN_LANES)
        ar = jnp.broadcast_to(ar_ref[:, re], (BATCH, SCAN_LANES))
        ai = jnp.broadcast_to(ai_ref[:, re], (BATCH, SCAN_LANES))

        def step(t, carry):
            sr, si = carry
            r = pl.ds(pl.multiple_of(t * BATCH, BATCH), BATCH)
            nr = ar * sr - ai * si + xs_ref[r, re]
            ni = ar * si + ai * sr + xs_ref[r, im]
            xs_ref[r, re] = nr
            xs_ref[r, im] = ni
            return nr, ni

        sr, si = lax.fori_loop(0, steps, step, (st_ref[:, re], st_ref[:, im]),
                               unroll=SCAN_UNROLL)
        st_ref[:, re] = sr
        st_ref[:, im] = si

    ys = []
    for m in range(S5_SPLIT):
        xr = xs_ref[:, m * S5_ST:(m + 1) * S5_ST].astype(BF16)
        xi = xs_ref[:, S_LANES + m * S5_ST:S_LANES + (m + 1) * S5_ST].astype(BF16)
        ys.append(_mm(xr, cblk_ref[m, 0:S5_ST, :]) + _mm(xi, cblk_ref[m, S5_ST:2 * S5_ST, :]))
    y = jnp.concatenate(ys, axis=1) + d_ref[...] * u
    y_bt = jnp.swapaxes(y.reshape(steps, BATCH, C_WIDTH), 0, 1).reshape(rows, C_WIDTH)
    gy = jax.nn.gelu(y_bt).astype(BF16)
    outs = []
    for c in range(0, D_MODEL, MXU_TILE):
        val = _mm(gy, wout_ref[:, c:c + MXU_TILE])
        gate = _mm(gy, wout_ref[:, D_MODEL + c:D_MODEL + c + MXU_TILE])
        xc = x_ref[:, :, c:c + MXU_TILE].reshape(rows, MXU_TILE)
        outs.append(xc + val * jax.nn.sigmoid(gate))
    o_ref[...] = jnp.concatenate(outs, axis=1).reshape(BATCH, steps, D_MODEL)


def _s5_discretize(lam_re, lam_im, log_dt, b_re, b_im, c_re, c_im):
    dt = jnp.exp(log_dt)[:, None]
    mag = jnp.exp(lam_re * dt)
    ar = mag * jnp.cos(lam_im * dt)
    ai = mag * jnp.sin(lam_im * dt)
    den = lam_re * lam_re + lam_im * lam_im
    qr = ((ar - 1.0) * lam_re + ai * lam_im) / den
    qi = (ai * lam_re - (ar - 1.0) * lam_im) / den
    bbr = qr[..., None] * b_re - qi[..., None] * b_im
    bbi = qr[..., None] * b_im + qi[..., None] * b_re
    gs = C_GROUPS // S5_SPLIT
    eye = jnp.eye(gs, dtype=F32)
    split = lambda a: a.reshape((S5_SPLIT, gs) + a.shape[1:])
    blk_b = lambda a: jnp.einsum('mgpc,gh->mgchp', split(a), eye).reshape(S5_SPLIT, S5_CH, S5_ST)
    blk_c = lambda a: jnp.einsum('mgcp,gh->mgphc', split(a), eye).reshape(S5_SPLIT, S5_ST, S5_CH)
    bblk = jnp.concatenate([blk_b(bbr), blk_b(bbi)], axis=2)
    cblk = jnp.concatenate([blk_c(c_re), -blk_c(c_im)], axis=1)
    return (bblk.astype(BF16), ar.reshape(1, S_LANES), ai.reshape(1, S_LANES), cblk.astype(BF16))


def _odd_layer(x, norm, w_in_all, ssm, d_skip, w_out_all, layer):
    bblk, ar, ai, cblk = ssm
    spec = pl.BlockSpec((BATCH, SCAN_STEPS, D_MODEL), lambda i: (0, i, 0))
    hbm = pl.BlockSpec(memory_space=pl.ANY)
    return pl.pallas_call(
        functools.partial(_odd_kernel, layer=layer),
        grid=(SEQ // SCAN_STEPS,),
        in_specs=[
            spec,
            _resident((1, D_MODEL)),
            hbm,
            _resident((S5_SPLIT, S5_CH, 2 * S5_ST)),
            _resident((1, S_LANES)),
            _resident((1, S_LANES)),
            _resident((S5_SPLIT, 2 * S5_ST, S5_CH)),
            _resident((1, C_WIDTH)),
            hbm,
        ],
        out_specs=spec,
        out_shape=jax.ShapeDtypeStruct((BATCH, SEQ, D_MODEL), F32),
        scratch_shapes=[pltpu.VMEM((SCAN_STEPS * BATCH, 2 * S_LANES), F32),
                        pltpu.VMEM((BATCH, 2 * S_LANES), F32),
                        pltpu.VMEM((D_MODEL, C_WIDTH), BF16),
                        pltpu.VMEM((C_WIDTH, 2 * D_MODEL), BF16),
                        _stage(CAST_ROWS, C_WIDTH),
                        _stage(CAST_ROWS_WIDE, 2 * D_MODEL),
                        pltpu.SemaphoreType.DMA((CAST_SLOTS,))],
        compiler_params=_params(1),
        name="odd_mixer",
    )(x, norm.reshape(1, -1), w_in_all, bblk, ar, ai, cblk, d_skip.reshape(1, -1), w_out_all)


def _ca_kernel(x_ref, m_ref, g_ref, gm_ref, wq_hbm, wk_hbm, wv_hbm, wo_hbm, o_ref,
               wq_ref, wk_ref, wv_ref, wo_ref, kt_ref, v_ref, stage, sem, *, layer):
    @pl.when((pl.program_id(0) == 0) & (pl.program_id(1) == 0))
    def _():
        _load_bf16_many([(wq_hbm.at[layer], wq_ref, CA_HEAD_DIM ** -0.5),
                         (wk_hbm.at[layer], wk_ref, None),
                         (wv_hbm.at[layer], wv_ref, None),
                         (wo_hbm.at[layer], wo_ref, None)], stage, sem)

    @pl.when(pl.program_id(1) == 0)
    def _():
        mn = _rms(m_ref[...], gm_ref[...]).astype(BF16)
        kt_ref[...] = _mm(mn, wk_ref[...]).T.astype(BF16)
        v_ref[...] = _mm(mn, wv_ref[...]).astype(BF16)

    rows = CA_TILE // CA_SPLIT
    for blk in range(CA_SPLIT):
        rs = slice(blk * rows, (blk + 1) * rows)
        x = x_ref[rs, :]
        q = _mm(_rms(x, g_ref[...]).astype(BF16), wq_ref[...]).astype(BF16)
        heads = []
        for h in range(CA_HEADS):
            hs = slice(h * CA_HEAD_DIM, (h + 1) * CA_HEAD_DIM)
            sc = _mm(q[:, hs], kt_ref[hs, :])
            e = jnp.exp(sc - jnp.max(sc, axis=-1, keepdims=True))
            heads.append(_mm(e.astype(BF16), v_ref[:, hs]) / jnp.sum(e, axis=-1, keepdims=True))
        o = jnp.concatenate(heads, axis=1).astype(BF16)
        o_ref[rs, :] = x_ref[rs, :] + _mm(o, wo_ref[...])


def _cross_attention(x, mem, norm, mem_norm, wq_all, wk_all, wv_all, wo_all, layer):
    mem_spec = pl.BlockSpec((None, MEM_LEN, D_MODEL), lambda b, s: (b, 0, 0))
    hbm = pl.BlockSpec(memory_space=pl.ANY)
    return pl.pallas_call(
        functools.partial(_ca_kernel, layer=layer),
        grid=(BATCH, SEQ // CA_TILE),
        in_specs=[_tok_spec(CA_TILE), mem_spec, _resident((1, D_MODEL)), _resident((1, D_MODEL)),
                  hbm, hbm, hbm, hbm],
        out_specs=_tok_spec(CA_TILE),
        out_shape=jax.ShapeDtypeStruct((BATCH, SEQ, D_MODEL), F32),
        scratch_shapes=([pltpu.VMEM((D_MODEL, D_MODEL), BF16)] * 4
                        + [pltpu.VMEM((D_MODEL, MEM_LEN), BF16),
                           pltpu.VMEM((MEM_LEN, D_MODEL), BF16),
                           _stage(CAST_ROWS, D_MODEL),
                           pltpu.SemaphoreType.DMA((CAST_SLOTS,))]),
        compiler_params=_params(2),
        name="cross_attention",
    )(x, mem, norm.reshape(1, -1), mem_norm.reshape(1, -1), wq_all, wk_all, wv_all, wo_all)


def _ffn_kernel(x_ref, g_ref, wg_hbm, wu_hbm, wd_hbm, fg_ref, o_ref,
                wg_ref, wu_ref, wd_ref, stage_in, stage_out, sem, *, layer, final):
    @pl.when((pl.program_id(0) == 0) & (pl.program_id(1) == 0))
    def _():
        _load_bf16_many([(wg_hbm.at[layer], wg_ref, None), (wu_hbm.at[layer], wu_ref, None)],
                        stage_in, sem)
        _load_bf16(wd_hbm.at[layer], wd_ref, stage_out, sem)

    xn = _rms(x_ref[...], g_ref[...]).astype(BF16)
    hs = []
    for n in range(FFN_HIDDEN // MXU_TILE):
        cs = slice(n * MXU_TILE, (n + 1) * MXU_TILE)
        hs.append((jax.nn.silu(_mm(xn, wg_ref[:, cs])) * _mm(xn, wu_ref[:, cs])).astype(BF16))
    y = x_ref[...] + _mm(jnp.concatenate(hs, axis=1), wd_ref[...])
    if final:
        y = _rms(y, fg_ref[...])
    o_ref[...] = y


def _ffn(x, norm, wg_all, wu_all, wd_all, final_norm, layer, final):
    hbm = pl.BlockSpec(memory_space=pl.ANY)
    return pl.pallas_call(
        functools.partial(_ffn_kernel, layer=layer, final=final),
        grid=(BATCH, SEQ // FFN_TILE),
        in_specs=[_tok_spec(FFN_TILE), _resident((1, D_MODEL)), hbm, hbm, hbm,
                  _resident((1, D_MODEL))],
        out_specs=_tok_spec(FFN_TILE),
        out_shape=jax.ShapeDtypeStruct((BATCH, SEQ, D_MODEL), F32),
        scratch_shapes=[pltpu.VMEM((D_MODEL, FFN_HIDDEN), BF16),
                        pltpu.VMEM((D_MODEL, FFN_HIDDEN), BF16),
                        pltpu.VMEM((FFN_HIDDEN, D_MODEL), BF16),
                        _stage(CAST_ROWS_WIDE, FFN_HIDDEN),
                        _stage(CAST_ROWS, D_MODEL),
                        pltpu.SemaphoreType.DMA((CAST_SLOTS,))],
        compiler_params=_params(2),
        name="ffn_final" if final else "ffn",
    )(x, norm.reshape(1, -1), wg_all, wu_all, wd_all, final_norm.reshape(1, -1))


def kernel(x, mem, e_norm, e_w_in, e_gmlp_w, e_gmlp_b, e_conv_w, e_conv_b, e_conv_ln_g, e_conv_ln_b, e_w_out, o_norm, o_w_in, o_lam_re, o_lam_im, o_log_dt, o_b_re, o_b_im, o_c_re, o_c_im, o_d, o_w_out, ca_norm, ca_mem_norm, ca_wq, ca_wk, ca_wv, ca_wo, ffn_norm, ffn_w_gate, ffn_w_up, ffn_w_down, final_norm):
    for i in range(DEPTH):
        j = i // 2
        if i % 2 == 0:
            x = _even_layer(x, e_norm[j], e_w_in, e_gmlp_w[j], e_gmlp_b[j], e_conv_w[j],
                            e_conv_b[j], e_conv_ln_g[j], e_conv_ln_b[j], e_w_out, layer=j)
        else:
            ssm = _s5_discretize(o_lam_re[j], o_lam_im[j], o_log_dt[j], o_b_re[j], o_b_im[j],
                                 o_c_re[j], o_c_im[j])
            x = _odd_layer(x, o_norm[j], o_w_in, ssm, o_d[j], o_w_out, layer=j)
        x = _cross_attention(x, mem, ca_norm[i], ca_mem_norm[i], ca_wq, ca_wk, ca_wv, ca_wo,
                             layer=i)
        x = _ffn(x, ffn_norm[i], ffn_w_gate, ffn_w_up, ffn_w_down, final_norm,
                 layer=i, final=(i == DEPTH - 1))
    return x
```
